```python
import jax, jax.numpy as jnp
from jax import lax
import numpy as np

D_MODEL = 1024
BATCH = 4
SEQ = 8192
DEPTH = 2

HEAD_DIM = 64
RMS_EPS = 1e-6
NEG_INF = -1e30
ATTN_SCALE = HEAD_DIM ** -0.5

POOL_WIDTH = 512
POOL_GROUPS = 4
POOL_GROUP_DIM = POOL_WIDTH // POOL_GROUPS
POOL_WINDOWS = (2, 4, 8, 16)

SWA_Q_HEADS = 8
SWA_KV_HEADS = 2
SWA_GROUP = SWA_Q_HEADS // SWA_KV_HEADS
SWA_WINDOW = 128
SWA_BLOCK = 128
SWA_WIDTH = SWA_Q_HEADS * HEAD_DIM
SWA_KV_WIDTH = SWA_KV_HEADS * HEAD_DIM

MOBA_HEADS = 8
MOBA_BLOCK = 256
MOBA_TOPK = 3
MOBA_Q_CHUNK = 64
MOBA_WIDTH = MOBA_HEADS * HEAD_DIM

N_BRANCH = 3
IN_SIZES = (POOL_WIDTH, POOL_WIDTH,
            SWA_WIDTH, SWA_KV_WIDTH, SWA_KV_WIDTH, SWA_WIDTH,
            MOBA_WIDTH, MOBA_WIDTH, MOBA_WIDTH, MOBA_WIDTH,
            N_BRANCH * D_MODEL)
IN_WIDTH = sum(IN_SIZES)

kernel_name = "hybrid_pool_swa_moba_gated_merge"


def _rmsnorm(x, g):
    x32 = x.astype(jnp.float32)
    y = x32 * lax.rsqrt(jnp.mean(x32 * x32, axis=-1, keepdims=True) + RMS_EPS)
    return (y * g.astype(jnp.float32)).astype(x.dtype)


def _pool_mixer(xa, w_pool, scale):
    b, s, _ = xa.shape
    x32 = xa.astype(jnp.float32)
    cs = jnp.cumsum(x32, axis=1)
    t = jnp.arange(s)
    outs = []
    for gi, w in enumerate(POOL_WINDOWS):
        cg = cs[..., gi * POOL_GROUP_DIM:(gi + 1) * POOL_GROUP_DIM]
        lag = jnp.pad(cg, ((0, 0), (w, 0), (0, 0)))[:, :s]
        cnt = jnp.minimum(t + 1, w).astype(jnp.float32)
        outs.append((cg - lag) / cnt[None, :, None])
    pooled = (jnp.concatenate(outs, axis=-1) - x32).astype(xa.dtype)
    pooled = pooled.reshape(b, s, POOL_GROUPS, POOL_GROUP_DIM)
    mixed = jnp.einsum('bsgc,gcd->bsgd', pooled, w_pool).reshape(b, s, POOL_WIDTH)
    return mixed * scale


def _swa_attention(q, k, v, sinks):
    b, s = q.shape[0], q.shape[1]
    nb = s // SWA_BLOCK
    qb = q.reshape(b, nb, SWA_BLOCK, SWA_KV_HEADS, SWA_GROUP, HEAD_DIM)
    kb = k.reshape(b, nb, SWA_BLOCK, SWA_KV_HEADS, HEAD_DIM)
    vb = v.reshape(b, nb, SWA_BLOCK, SWA_KV_HEADS, HEAD_DIM)

    def with_prev(t):
        prev = jnp.pad(t[:, :-1], ((0, 0), (1, 0), (0, 0), (0, 0), (0, 0)))
        return jnp.concatenate([prev, t], axis=2)

    kk = with_prev(kb)
    vv = with_prev(vb)
    sc = jnp.einsum('bnqhgd,bnkhd->bnhgqk', qb, kk,
                    preferred_element_type=jnp.float32) * ATTN_SCALE
    qi = jnp.arange(SWA_BLOCK)[:, None] + SWA_BLOCK
    kj = jnp.arange(2 * SWA_BLOCK)[None, :]
    diff = qi - kj
    band = (diff >= 0) & (diff < SWA_WINDOW)
    kpos = jnp.arange(nb)[:, None, None] * SWA_BLOCK + kj[None] - SWA_BLOCK
    valid = band[None] & (kpos >= 0)
    sc = jnp.where(valid[None, :, None, None], sc, NEG_INF)
    sink = sinks.astype(jnp.float32).reshape(1, 1, SWA_KV_HEADS, SWA_GROUP, 1)
    m = jnp.maximum(jnp.max(sc, axis=-1), sink)
    p = jnp.exp(sc - m[..., None])
    denom = jnp.sum(p, axis=-1) + jnp.exp(sink - m)
    p = (p / denom[..., None]).astype(v.dtype)
    o = jnp.einsum('bnhgqk,bnkhd->bnqhgd', p, vv)
    return o.reshape(b, s, SWA_WIDTH)


def _moba_attention(q, k, v):
    b, s = q.shape[0], q.shape[1]
    nblk = -(-s // MOBA_BLOCK)
    s_pad = nblk * MOBA_BLOCK
    pad = ((0, 0), (0, s_pad - s), (0, 0), (0, 0))
    q = jnp.pad(q, pad)
    k = jnp.pad(k, pad)
    v = jnp.pad(v, pad)
    topk = min(MOBA_TOPK, nblk)
    kb = k.reshape(b, nblk, MOBA_BLOCK, MOBA_HEADS, HEAD_DIM)
    vb = v.reshape(b, nblk, MOBA_BLOCK, MOBA_HEADS, HEAD_DIM)
    k_mean = jnp.mean(kb.astype(jnp.float32), axis=2)
    k_bh = kb.transpose(0, 3, 1, 2, 4)
    v_bh = vb.transpose(0, 3, 1, 2, 4)
    bi = jnp.arange(b)[:, None, None, None]
    hi = jnp.arange(MOBA_HEADS)[None, None, :, None]
    n_chunks = s_pad // MOBA_Q_CHUNK

    def chunk_fn(c):
        q0 = c * MOBA_Q_CHUNK
        blk = q0 // MOBA_BLOCK
        qpos = q0 + jnp.arange(MOBA_Q_CHUNK)
        qc = lax.dynamic_slice_in_dim(q, q0, MOBA_Q_CHUNK, axis=1)
        gate = jnp.einsum('bqhd,bnhd->bqhn', qc.astype(jnp.float32), k_mean)
        past = jnp.arange(nblk) < blk
        gate = jnp.where(past[None, None, None, :], gate, NEG_INF)
        _, idx = lax.top_k(gate, topk)
        slot_valid = jnp.arange(topk) < blk
        k_sel = k_bh[bi, hi, idx]
        v_sel = v_bh[bi, hi, idx]
        s_sel = jnp.einsum('bqhd,bqhjkd->bqhjk', qc, k_sel,
                           preferred_element_type=jnp.float32) * ATTN_SCALE
        s_sel = jnp.where(slot_valid[:, None], s_sel, NEG_INF)
        s_sel = s_sel.reshape(b, MOBA_Q_CHUNK, MOBA_HEADS, topk * MOBA_BLOCK)
        k_own = lax.dynamic_slice_in_dim(kb, blk, 1, axis=1)[:, 0]
        v_own = lax.dynamic_slice_in_dim(vb, blk, 1, axis=1)[:, 0]
        s_own = jnp.einsum('bqhd,bkhd->bqhk', qc, k_own,
                           preferred_element_type=jnp.float32) * ATTN_SCALE
        kpos = blk * MOBA_BLOCK + jnp.arange(MOBA_BLOCK)
        causal = kpos[None, :] <= qpos[:, None]
        s_own = jnp.where(causal[None, :, None, :], s_own, NEG_INF)
        p = jax.nn.softmax(jnp.concatenate([s_sel, s_own], axis=-1), axis=-1).astype(v.dtype)
        p_sel = p[..., :topk * MOBA_BLOCK].reshape(b, MOBA_Q_CHUNK, MOBA_HEADS, topk, MOBA_BLOCK)
        p_own = p[..., topk * MOBA_BLOCK:]
        o = (jnp.einsum('bqhjk,bqhjkd->bqhd', p_sel, v_sel)
             + jnp.einsum('bqhk,bkhd->bqhd', p_own, v_own))
        return o.astype(q.dtype)

    outs = lax.map(chunk_fn, jnp.arange(n_chunks))
    o = outs.transpose(1, 0, 2, 3, 4).reshape(b, s_pad, MOBA_WIDTH)
    return o[:, :s]


def _layer(x, norm_g, w_in, pool_w, pool_scale, sinks, w_proj_a, w_proj_b, w_proj_c, w_out):
    b, s, _ = x.shape
    h = _rmsnorm(x, norm_g)
    u = h @ w_in
    offsets = np.cumsum(IN_SIZES)[:-1].tolist()
    (a_x, a_z, b_q, b_k, b_v, b_z, c_q, c_k, c_v, c_z, gate_logits) = jnp.split(u, offsets, axis=-1)
    ya = _pool_mixer(a_x, pool_w, pool_scale) * jax.nn.silu(a_z)
    yb = _swa_attention(b_q.reshape(b, s, SWA_Q_HEADS, HEAD_DIM),
                        b_k.reshape(b, s, SWA_KV_HEADS, HEAD_DIM),
                        b_v.reshape(b, s, SWA_KV_HEADS, HEAD_DIM), sinks) * jax.nn.silu(b_z)
    yc = _moba_attention(c_q.reshape(b, s, MOBA_HEADS, HEAD_DIM),
                         c_k.reshape(b, s, MOBA_HEADS, HEAD_DIM),
                         c_v.reshape(b, s, MOBA_HEADS, HEAD_DIM)) * jax.nn.silu(c_z)
    ga, gb, gc = jnp.split(jax.nn.sigmoid(gate_logits), N_BRANCH, axis=-1)
    merged = ga * (ya @ w_proj_a) + gb * (yb @ w_proj_b) + gc * (yc @ w_proj_c)
    return x + merged @ w_out


def setup_inputs(seed: int = 0) -> dict:
    key = jax.random.key(seed)
    ks = jax.random.split(key, 11)
    nrm = jax.random.normal
    x = nrm(ks[0], (BATCH, SEQ, D_MODEL), jnp.float32)
    norm_g = 1.0 + 0.05 * nrm(ks[1], (DEPTH, D_MODEL), jnp.float32)
    w_in = nrm(ks[2], (DEPTH, D_MODEL, IN_WIDTH), jnp.float32) * D_MODEL ** -0.5
    pool_w = nrm(ks[3], (DEPTH, POOL_GROUPS, POOL_GROUP_DIM, POOL_GROUP_DIM), jnp.float32) * POOL_GROUP_DIM ** -0.5
    pool_scale = 1.0 + 0.1 * nrm(ks[4], (DEPTH, POOL_WIDTH), jnp.float32)
    sink_logits = 0.5 * nrm(ks[5], (DEPTH, SWA_Q_HEADS), jnp.float32)
    w_proj_a = nrm(ks[6], (DEPTH, POOL_WIDTH, D_MODEL), jnp.float32) * POOL_WIDTH ** -0.5
    w_proj_b = nrm(ks[7], (DEPTH, SWA_WIDTH, D_MODEL), jnp.float32) * SWA_WIDTH ** -0.5
    w_proj_c = nrm(ks[8], (DEPTH, MOBA_WIDTH, D_MODEL), jnp.float32) * MOBA_WIDTH ** -0.5
    w_out = nrm(ks[9], (DEPTH, D_MODEL, D_MODEL), jnp.float32) * D_MODEL ** -0.5
    final_norm_g = 1.0 + 0.05 * nrm(ks[10], (D_MODEL,), jnp.float32)
    return {"x": x, "norm_g": norm_g, "w_in": w_in, "pool_w": pool_w, "pool_scale": pool_scale,
            "sink_logits": sink_logits, "w_proj_a": w_proj_a, "w_proj_b": w_proj_b,
            "w_proj_c": w_proj_c, "w_out": w_out, "final_norm_g": final_norm_g}


def reference(x, norm_g, w_in, pool_w, pool_scale, sink_logits, w_proj_a, w_proj_b, w_proj_c,
              w_out, final_norm_g):
    for l in range(DEPTH):
        x = _layer(x, norm_g[l], w_in[l], pool_w[l], pool_scale[l], sink_logits[l],
                   w_proj_a[l], w_proj_b[l], w_proj_c[l], w_out[l])
    return _rmsnorm(x, final_norm_g)
```

```python
import functools

import numpy as np
import jax
import jax.numpy as jnp
from jax import lax
from jax.experimental import pallas as pl
from jax.experimental.pallas import tpu as pltpu

F32 = jnp.float32
BF16 = jnp.bfloat16

LANES = 128
V7X_VMEM_LIMIT_BYTES = 56 * 1024 * 1024

D_MODEL = 1024
HEAD_DIM = 64
RMS_EPS = 1e-6
NEG_INF = -1e30
LOG2E = 1.4426950408889634
ATTN_SCALE = HEAD_DIM ** -0.5

POOL_WIDTH = 512
POOL_GROUPS = 4
POOL_GROUP_DIM = POOL_WIDTH // POOL_GROUPS
POOL_WINDOWS = (2, 4, 8, 16)
POOL_HALO = 16

SWA_Q_HEADS = 8
SWA_KV_HEADS = 2
SWA_BLOCK = 128
SWA_WIDTH = SWA_Q_HEADS * HEAD_DIM
SWA_KV_WIDTH = SWA_KV_HEADS * HEAD_DIM

MOBA_HEADS = 8
MOBA_BLOCK = 256
MOBA_TOPK = 3
MOBA_WIDTH = MOBA_HEADS * HEAD_DIM

N_BRANCH = 3

_REF_SIZES = (POOL_WIDTH, POOL_WIDTH, SWA_WIDTH, SWA_KV_WIDTH, SWA_KV_WIDTH, SWA_WIDTH,
              MOBA_WIDTH, MOBA_WIDTH, MOBA_WIDTH, MOBA_WIDTH, N_BRANCH * D_MODEL)
_REF_OFF = dict(zip(("a_x", "a_z", "b_q", "b_k", "b_v", "b_z", "c_q", "c_k", "c_v", "c_z", "gates"),
                    np.cumsum((0,) + _REF_SIZES[:-1]).tolist()))

_SWA_HEAD_ORDER = (0, 4, 1, 5, 2, 6, 3, 7)

U_COL = dict(a_x=0, a_z=512, b_q=1024, b_z=1536, c_q=2048, c_k=2560, c_v=3072, c_z=3584,
             gates=4096, b_kv=7168)
U_WIDTH = 7168 + 2 * SWA_KV_WIDTH


def _u_permutation():
    def rng(name, n):
        return np.arange(_REF_OFF[name], _REF_OFF[name] + n)

    def swa_heads(name):
        base = _REF_OFF[name]
        return np.concatenate([np.arange(base + h * HEAD_DIM, base + (h + 1) * HEAD_DIM)
                               for h in _SWA_HEAD_ORDER])

    cols = np.concatenate([
        rng("a_x", POOL_WIDTH), rng("a_z", POOL_WIDTH),
        swa_heads("b_q"), swa_heads("b_z"),
        rng("c_q", MOBA_WIDTH), rng("c_k", MOBA_WIDTH), rng("c_v", MOBA_WIDTH), rng("c_z", MOBA_WIDTH),
        rng("gates", N_BRANCH * D_MODEL),
        rng("b_k", SWA_KV_WIDTH), rng("b_v", SWA_KV_WIDTH)])
    scale = np.ones((U_WIDTH,), np.float32)
    qscale = np.float32(ATTN_SCALE * LOG2E)
    scale[U_COL["b_q"]:U_COL["b_q"] + SWA_WIDTH] = qscale
    scale[U_COL["c_q"]:U_COL["c_q"] + MOBA_WIDTH] = qscale
    assert cols.shape == (U_WIDTH,)
    return cols, scale


_U_COLS, _U_SCALE = _u_permutation()


def _silu(z):
    return z / (1.0 + jnp.exp(-z))


def _sigmoid(z):
    return 1.0 / (1.0 + jnp.exp(-z))


_NT = (((1,), (1,)), ((), ()))
_TN = (((0,), (0,)), ((), ()))


INPROJ_TM = 512
INPROJ_CHUNK = 1024


def _inproj_kernel(x_ref, g_ref, w_ref, u_ref):
    x = x_ref[0]
    ms = jnp.mean(x * x, axis=-1, keepdims=True)
    h = (x * lax.rsqrt(ms + RMS_EPS) * g_ref[...]).astype(BF16)
    for c0 in range(0, U_WIDTH, INPROJ_CHUNK):
        c1 = min(c0 + INPROJ_CHUNK, U_WIDTH)
        u_ref[0, :, c0:c1] = jnp.dot(h, w_ref[:, c0:c1],
                                     preferred_element_type=F32).astype(BF16)


def _inproj(x, g, w):
    b, s, d = x.shape
    return pl.pallas_call(
        _inproj_kernel,
        grid=(b, s // INPROJ_TM),
        in_specs=[pl.BlockSpec((1, INPROJ_TM, d), lambda bi, i: (bi, i, 0)),
                  pl.BlockSpec((1, d), lambda bi, i: (0, 0)),
                  pl.BlockSpec((d, U_WIDTH), lambda bi, i: (0, 0))],
        out_specs=pl.BlockSpec((1, INPROJ_TM, U_WIDTH), lambda bi, i: (bi, i, 0)),
        out_shape=jax.ShapeDtypeStruct((b, s, U_WIDTH), BF16),
        compiler_params=pltpu.CompilerParams(
            dimension_semantics=("arbitrary", "arbitrary"),
            vmem_limit_bytes=V7X_VMEM_LIMIT_BYTES),
        name="inproj",
    )(x, g, w)


SWA_TS = 512
SWA_TILES = SWA_WIDTH // LANES


def _swa_kernel(q_ref, z_ref, kv_ref, kvh_ref, sink_ref, o_ref):
    i = pl.program_id(1)
    nq = SWA_TILES * SWA_BLOCK
    lane = lax.broadcasted_iota(jnp.int32, (SWA_BLOCK, LANES), 1)
    half_mask = (lane < HEAD_DIM, lane >= HEAD_DIM)
    r = lax.broadcasted_iota(jnp.int32, (2 * SWA_BLOCK, nq), 0)
    c = lax.broadcasted_iota(jnp.int32, (2 * SWA_BLOCK, nq), 1) & (SWA_BLOCK - 1)
    band = (r > c) & (r <= c + SWA_BLOCK)
    for n in range(SWA_TS // SWA_BLOCK):
        rows = slice(n * SWA_BLOCK, (n + 1) * SWA_BLOCK)
        if n == 0:
            kv_prev = kvh_ref[0]
            valid = band & ((r >= SWA_BLOCK) | (i > 0))
        else:
            kv_prev = kv_ref[0, (n - 1) * SWA_BLOCK:n * SWA_BLOCK, :]
            valid = band
        kv2 = jnp.concatenate([kv_prev, kv_ref[0, rows, :]], axis=0)
        k2 = kv2[:, :SWA_KV_WIDTH]
        v2 = kv2[:, SWA_KV_WIDTH:]
        halves = []
        for half in range(SWA_KV_HEADS):
            qs = jnp.concatenate(
                [jnp.where(half_mask[half], q_ref[0, rows, t * LANES:(t + 1) * LANES], 0)
                 for t in range(SWA_TILES)], axis=0)
            st = lax.dot_general(k2, qs, _NT, preferred_element_type=F32)
            st = jnp.where(valid, st, NEG_INF)
            sink = sink_ref[half:half + 1, :]
            m = jnp.maximum(jnp.max(st, axis=0, keepdims=True), sink)
            p = jnp.exp2(st - m)
            denom = jnp.sum(p, axis=0, keepdims=True) + jnp.exp2(sink - m)
            ot = lax.dot_general(v2, p.astype(BF16), _TN, preferred_element_type=F32)
            halves.append(ot[half * HEAD_DIM:(half + 1) * HEAD_DIM] / denom)
        ot = jnp.concatenate(halves, axis=0)
        for t in range(SWA_TILES):
            cols = slice(t * LANES, (t + 1) * LANES)
            o = ot[:, cols].T
            z = z_ref[0, rows, cols].astype(F32)
            o_ref[0, rows, cols] = (o * _silu(z)).astype(BF16)


def _swa(u, sink2):
    b, s, _ = u.shape
    blk = lambda name, w: U_COL[name] // w
    halo_blocks = SWA_TS // SWA_BLOCK
    return pl.pallas_call(
        _swa_kernel,
        grid=(b, s // SWA_TS),
        in_specs=[
            pl.BlockSpec((1, SWA_TS, SWA_WIDTH), lambda bi, i: (bi, i, blk("b_q", SWA_WIDTH))),
            pl.BlockSpec((1, SWA_TS, SWA_WIDTH), lambda bi, i: (bi, i, blk("b_z", SWA_WIDTH))),
            pl.BlockSpec((1, SWA_TS, 2 * SWA_KV_WIDTH),
                         lambda bi, i: (bi, i, blk("b_kv", 2 * SWA_KV_WIDTH))),
            pl.BlockSpec((1, SWA_BLOCK, 2 * SWA_KV_WIDTH),
                         lambda bi, i: (bi, jnp.maximum(i * halo_blocks - 1, 0),
                                        blk("b_kv", 2 * SWA_KV_WIDTH))),
            pl.BlockSpec((SWA_KV_HEADS, SWA_TILES * SWA_BLOCK), lambda bi, i: (0, 0)),
        ],
        out_specs=pl.BlockSpec((1, SWA_TS, SWA_WIDTH), lambda bi, i: (bi, i, 0)),
        out_shape=jax.ShapeDtypeStruct((b, s, SWA_WIDTH), BF16),
        compiler_params=pltpu.CompilerParams(
            dimension_semantics=("arbitrary", "arbitrary"),
            vmem_limit_bytes=V7X_VMEM_LIMIT_BYTES),
        name="swa",
    )(u, u, u, u, sink2)


MOBA_PAIRS = MOBA_WIDTH // LANES


def _moba_kernel(q_ref, z_ref, k_ref, v_ref, o_ref, kmean_ref, sel_ref, *, nblk):
    i = pl.program_id(2)

    @pl.when(i == 0)
    def _():
        def mean_body(j, carry):
            off = pl.multiple_of(j * MOBA_BLOCK, MOBA_BLOCK)
            kj = k_ref[0, pl.ds(off, MOBA_BLOCK), :].astype(F32)
            kmean_ref[pl.ds(j, 1), :] = jnp.sum(kj, axis=0, keepdims=True) * (1.0 / MOBA_BLOCK)
            return carry
        lax.fori_loop(0, nblk, mean_body, 0)

    q = q_ref[0]
    lane = lax.broadcasted_iota(jnp.int32, (MOBA_BLOCK, LANES), 1)
    q_heads = (jnp.where(lane < HEAD_DIM, q, 0), jnp.where(lane >= HEAD_DIM, q, 0))

    km = kmean_ref[...]
    km_hi = km.astype(BF16)
    km_lo = (km - km_hi.astype(F32)).astype(BF16)
    blk_id = lax.broadcasted_iota(jnp.int32, (nblk, MOBA_BLOCK), 0).astype(F32)
    past = blk_id < i.astype(F32)
    for h in range(2):
        g = (lax.dot_general(km_hi, q_heads[h], _NT, preferred_element_type=F32)
             + lax.dot_general(km_lo, q_heads[h], _NT, preferred_element_type=F32))
        g = jnp.where(past, g, NEG_INF)
        sel = jnp.zeros((nblk, MOBA_BLOCK), jnp.bool_)
        for _ in range(MOBA_TOPK):
            mx = jnp.max(g, axis=0, keepdims=True)
            first = jnp.min(jnp.where(g == mx, blk_id, float(nblk)), axis=0, keepdims=True)
            pick = blk_id == first
            sel = sel | pick
            g = jnp.where(pick, -jnp.inf, g)
        sel_ref[h] = jnp.where(sel & past, 1.0, 0.0)

    def attend(kj, vj, on, mask, state):
        new_state = []
        for h in range(2):
            m, l, acc = state[h]
            st = lax.dot_general(kj, q_heads[h], _NT, preferred_element_type=F32)
            if mask is not None:
                st = jnp.where(mask, st, NEG_INF)
            mb = jnp.max(st, axis=0, keepdims=True)
            if on is None:
                m_new = jnp.maximum(m, mb)
                m_use = m_new
            else:
                m_new = jnp.maximum(m, jnp.where(on[h] > 0.5, mb, NEG_INF))
                m_use = jnp.where(on[h] > 0.5, m_new, mb)
            p = jnp.exp2(st - m_use)
            lb = jnp.sum(p, axis=0, keepdims=True)
            pv = lax.dot_general(vj, p.astype(BF16), _TN, preferred_element_type=F32)
            pv = pv[h * HEAD_DIM:(h + 1) * HEAD_DIM]
            alpha = jnp.exp2(m - m_new)
            if on is not None:
                lb = lb * on[h]
                pv = pv * on[h]
            new_state.append((m_new, alpha * l + lb, alpha * acc + pv))
        return tuple(new_state)

    def body(j, state):
        off = pl.multiple_of(j * MOBA_BLOCK, MOBA_BLOCK)
        kj = k_ref[0, pl.ds(off, MOBA_BLOCK), :]
        vj = v_ref[0, pl.ds(off, MOBA_BLOCK), :]
        on = tuple(sel_ref[h, pl.ds(j, 1), :] for h in range(2))
        return attend(kj, vj, on, None, state)

    init = tuple((jnp.full((1, MOBA_BLOCK), NEG_INF, F32),
                  jnp.zeros((1, MOBA_BLOCK), F32),
                  jnp.zeros((HEAD_DIM, MOBA_BLOCK), F32)) for _ in range(2))
    state = lax.fori_loop(0, i, body, init)

    off = pl.multiple_of(i * MOBA_BLOCK, MOBA_BLOCK)
    kr = lax.broadcasted_iota(jnp.int32, (MOBA_BLOCK, MOBA_BLOCK), 0)
    qc = lax.broadcasted_iota(jnp.int32, (MOBA_BLOCK, MOBA_BLOCK), 1)
    state = attend(k_ref[0, pl.ds(off, MOBA_BLOCK), :], v_ref[0, pl.ds(off, MOBA_BLOCK), :],
                   None, kr <= qc, state)

    ot = jnp.concatenate([acc / l for (_, l, acc) in state], axis=0)
    z = z_ref[0].astype(F32)
    o_ref[0] = (ot.T * _silu(z)).astype(BF16)


def _moba(u):
    b, s, _ = u.shape
    nblk = s // MOBA_BLOCK
    blk = lambda name: U_COL[name] // LANES
    return pl.pallas_call(
        functools.partial(_moba_kernel, nblk=nblk),
        grid=(b, MOBA_PAIRS, nblk),
        in_specs=[
            pl.BlockSpec((1, MOBA_BLOCK, LANES), lambda bi, p, i: (bi, i, blk("c_q") + p)),
            pl.BlockSpec((1, MOBA_BLOCK, LANES), lambda bi, p, i: (bi, i, blk("c_z") + p)),
            pl.BlockSpec((1, s, LANES), lambda bi, p, i: (bi, 0, blk("c_k") + p)),
            pl.BlockSpec((1, s, LANES), lambda bi, p, i: (bi, 0, blk("c_v") + p)),
        ],
        out_specs=pl.BlockSpec((1, MOBA_BLOCK, LANES), lambda bi, p, i: (bi, i, p)),
        out_shape=jax.ShapeDtypeStruct((b, s, MOBA_WIDTH), BF16),
        scratch_shapes=[pltpu.VMEM((nblk, LANES), F32),
                        pltpu.VMEM((2, nblk, MOBA_BLOCK), F32)],
        compiler_params=pltpu.CompilerParams(
            dimension_semantics=("arbitrary", "arbitrary", "arbitrary"),
            vmem_limit_bytes=V7X_VMEM_LIMIT_BYTES),
        name="moba",
    )(u, u, u, u)


MERGE_TM = 512


def _merge_kernel(x_ref, ax_ref, axh_ref, az_ref, yb_ref, yc_ref, ga_ref, gb_ref, gc_ref,
                  pw_ref, ps_ref, wa_ref, wb_ref, wc_ref, wo_ref, fg_ref, o_ref, buf_ref,
                  *, final_norm):
    i = pl.program_id(1)
    tm = MERGE_TM
    buf_ref[0:POOL_HALO, :] = jnp.where(i > 0, axh_ref[0].astype(F32), 0.0)
    buf_ref[POOL_HALO:POOL_HALO + tm, :] = ax_ref[0].astype(F32)

    tpos = i * tm + lax.broadcasted_iota(jnp.int32, (tm, POOL_GROUP_DIM), 0)
    ya = []
    for gi, w in enumerate(POOL_WINDOWS):
        cols = slice(gi * POOL_GROUP_DIM, (gi + 1) * POOL_GROUP_DIM)
        cur = buf_ref[POOL_HALO:POOL_HALO + tm, cols]
        win = cur
        for sft in range(1, w):
            win = win + buf_ref[POOL_HALO - sft:POOL_HALO - sft + tm, cols]
        cnt = jnp.minimum(tpos + 1, w).astype(F32)
        pooled = (win / cnt - cur).astype(BF16)
        mixed = jnp.dot(pooled, pw_ref[gi], preferred_element_type=F32) * ps_ref[:, cols]
        ya.append((mixed * _silu(az_ref[0, :, cols].astype(F32))).astype(BF16))
    ya = jnp.concatenate(ya, axis=1)

    merged = _sigmoid(ga_ref[0].astype(F32)) * jnp.dot(ya, wa_ref[...], preferred_element_type=F32)
    merged += _sigmoid(gb_ref[0].astype(F32)) * jnp.dot(yb_ref[0], wb_ref[...],
                                                        preferred_element_type=F32)
    merged += _sigmoid(gc_ref[0].astype(F32)) * jnp.dot(yc_ref[0], wc_ref[...],
                                                        preferred_element_type=F32)
    out = x_ref[0] + jnp.dot(merged.astype(BF16), wo_ref[...], preferred_element_type=F32)
    if final_norm:
        ms = jnp.mean(out * out, axis=-1, keepdims=True)
        out = out * lax.rsqrt(ms + RMS_EPS) * fg_ref[...]
    o_ref[0] = out


def _merge(x, u, yb, yc, pool_w, pool_scale, wa, wb, wc, wo, fg, final_norm):
    b, s, d = x.shape
    tm = MERGE_TM
    ublk = lambda name, w: U_COL[name] // w
    gate_blk = U_COL["gates"] // d
    const = lambda shape: pl.BlockSpec(shape, lambda bi, i: (0,) * len(shape))
    row = lambda w, c: pl.BlockSpec((1, tm, w), lambda bi, i: (bi, i, c))
    return pl.pallas_call(
        functools.partial(_merge_kernel, final_norm=final_norm),
        grid=(b, s // tm),
        in_specs=[
            row(d, 0),
            row(POOL_WIDTH, ublk("a_x", POOL_WIDTH)),
            pl.BlockSpec((1, POOL_HALO, POOL_WIDTH),
                         lambda bi, i: (bi, jnp.maximum(i * (tm // POOL_HALO) - 1, 0),
                                        ublk("a_x", POOL_WIDTH))),
            row(POOL_WIDTH, ublk("a_z", POOL_WIDTH)),
            row(SWA_WIDTH, 0),
            row(MOBA_WIDTH, 0),
            row(d, gate_blk), row(d, gate_blk + 1), row(d, gate_blk + 2),
            const((POOL_GROUPS, POOL_GROUP_DIM, POOL_GROUP_DIM)),
            const((1, POOL_WIDTH)),
            const((POOL_WIDTH, d)), const((SWA_WIDTH, d)), const((MOBA_WIDTH, d)),
            const((d, d)),
            const((1, d)),
        ],
        out_specs=row(d, 0),
        out_shape=jax.ShapeDtypeStruct((b, s, d), F32),
        scratch_shapes=[pltpu.VMEM((POOL_HALO + tm, POOL_WIDTH), F32)],
        compiler_params=pltpu.CompilerParams(
            dimension_semantics=("arbitrary", "arbitrary"),
            vmem_limit_bytes=V7X_VMEM_LIMIT_BYTES),
        name="merge",
    )(x, u, u, u, yb, yc, u, u, u, pool_w, pool_scale, wa, wb, wc, wo, fg)


def kernel(x, norm_g, w_in, pool_w, pool_scale, sink_logits, w_proj_a, w_proj_b, w_proj_c,
           w_out, final_norm_g):
    depth = w_in.shape[0]
    assert x.shape[1] % max(INPROJ_TM, SWA_TS, MOBA_BLOCK, MERGE_TM) == 0
    assert x.shape[2] == D_MODEL and w_in.shape[2] == sum(_REF_SIZES)
    head_order = np.asarray(_SWA_HEAD_ORDER)
    swa_rows = (head_order[:, None] * HEAD_DIM + np.arange(HEAD_DIM)[None, :]).reshape(-1)
    fg = final_norm_g.reshape(1, D_MODEL)
    for l in range(depth):
        w_u = (w_in[l][:, _U_COLS] * _U_SCALE[None, :]).astype(BF16)
        sink2 = (sink_logits[l].astype(F32) * LOG2E)[
            np.arange(SWA_TILES)[None, :] + SWA_TILES * np.arange(SWA_KV_HEADS)[:, None]]
        sink2 = jnp.repeat(sink2, SWA_BLOCK, axis=1)
        u = _inproj(x, norm_g[l].reshape(1, D_MODEL), w_u)
        yb = _swa(u, sink2)
        yc = _moba(u)
        x = _merge(x, u, yb, yc,
                   pool_w[l].astype(BF16), pool_scale[l].reshape(1, POOL_WIDTH),
                   w_proj_a[l].astype(BF16), w_proj_b[l][swa_rows].astype(BF16),
                   w_proj_c[l].astype(BF16), w_out[l].astype(BF16), fg,
                   final_norm=(l == depth - 1))
    return x
```

```python
import functools

import numpy as np
import jax
import jax.numpy as jnp
from jax import lax
from jax.experimental import pallas as pl
from jax.experimental.pallas import tpu as pltpu

F32 = jnp.float32
BF16 = jnp.bfloat16

LANES = 128
V7X_VMEM_LIMIT_BYTES = 56 * 1024 * 1024

D_MODEL = 1024
HEAD_DIM = 64
RMS_EPS = 1e-6
NEG_INF = -1e30
LOG2E = 1.4426950408889634
ATTN_SCALE = HEAD_DIM ** -0.5

POOL_WIDTH = 512
POOL_GROUPS = 4
POOL_GROUP_DIM = POOL_WIDTH // POOL_GROUPS
POOL_WINDOWS = (2, 4, 8, 16)
POOL_HALO = 16

SWA_Q_HEADS = 8
SWA_KV_HEADS = 2
SWA_BLOCK = 128
SWA_WIDTH = SWA_Q_HEADS * HEAD_DIM
SWA_KV_WIDTH = SWA_KV_HEADS * HEAD_DIM

MOBA_HEADS = 8
MOBA_BLOCK = 256
MOBA_TOPK = 3
MOBA_WIDTH = MOBA_HEADS * HEAD_DIM

N_BRANCH = 3

_REF_SIZES = (POOL_WIDTH, POOL_WIDTH, SWA_WIDTH, SWA_KV_WIDTH, SWA_KV_WIDTH, SWA_WIDTH,
              MOBA_WIDTH, MOBA_WIDTH, MOBA_WIDTH, MOBA_WIDTH, N_BRANCH * D_MODEL)
_REF_OFF = dict(zip(("a_x", "a_z", "b_q", "b_k", "b_v", "b_z", "c_q", "c_k", "c_v", "c_z", "gates"),
                    np.cumsum((0,) + _REF_SIZES[:-1]).tolist()))

_SWA_HEAD_ORDER = (0, 4, 1, 5, 2, 6, 3, 7)

U_COL = dict(a_x=0, a_z=512, b_q=1024, b_z=1536, c_q=2048, c_k=2560, c_v=3072, c_z=3584,
             gates=4096, b_kv=7168)
U_WIDTH = 7168 + 2 * SWA_KV_WIDTH


def _u_permutation():
    def rng(name, n):
        return np.arange(_REF_OFF[name], _REF_OFF[name] + n)

    def swa_heads(name):
        base = _REF_OFF[name]
        return np.concatenate([np.arange(base + h * HEAD_DIM, base + (h + 1) * HEAD_DIM)
                               for h in _SWA_HEAD_ORDER])

    cols = np.concatenate([
        rng("a_x", POOL_WIDTH), rng("a_z", POOL_WIDTH),
        swa_heads("b_q"), swa_heads("b_z"),
        rng("c_q", MOBA_WIDTH), rng("c_k", MOBA_WIDTH), rng("c_v", MOBA_WIDTH), rng("c_z", MOBA_WIDTH),
        rng("gates", N_BRANCH * D_MODEL),
        rng("b_k", SWA_KV_WIDTH), rng("b_v", SWA_KV_WIDTH)])
    scale = np.ones((U_WIDTH,), np.float32)
    qscale = np.float32(ATTN_SCALE * LOG2E)
    scale[U_COL["b_q"]:U_COL["b_q"] + SWA_WIDTH] = qscale
    scale[U_COL["c_q"]:U_COL["c_q"] + MOBA_WIDTH] = qscale
    assert cols.shape == (U_WIDTH,)
    return cols, scale


_U_COLS, _U_SCALE = _u_permutation()


def _silu(z):
    return z / (1.0 + jnp.exp(-z))


def _sigmoid(z):
    return 1.0 / (1.0 + jnp.exp(-z))


_NT = (((1,), (1,)), ((), ()))
_TN = (((0,), (0,)), ((), ()))


INPROJ_TM = 512
INPROJ_CHUNK = 1024


def _inproj_kernel(x_ref, g_ref, w_ref, u_ref):
    x = x_ref[0]
    ms = jnp.mean(x * x, axis=-1, keepdims=True)
    h = (x * lax.rsqrt(ms + RMS_EPS) * g_ref[...]).astype(BF16)
    for c0 in range(0, U_WIDTH, INPROJ_CHUNK):
        c1 = min(c0 + INPROJ_CHUNK, U_WIDTH)
        u_ref[0, :, c0:c1] = jnp.dot(h, w_ref[:, c0:c1],
                                     preferred_element_type=F32).astype(BF16)


def _inproj(x, g, w):
    b, s, d = x.shape
    return pl.pallas_call(
        _inproj_kernel,
        grid=(b, s // INPROJ_TM),
        in_specs=[pl.BlockSpec((1, INPROJ_TM, d), lambda bi, i: (bi, i, 0)),
                  pl.BlockSpec((1, d), lambda bi, i: (0, 0)),
                  pl.BlockSpec((d, U_WIDTH), lambda bi, i: (0, 0))],
        out_specs=pl.BlockSpec((1, INPROJ_TM, U_WIDTH), lambda bi, i: (bi, i, 0)),
        out_shape=jax.ShapeDtypeStruct((b, s, U_WIDTH), BF16),
        compiler_params=pltpu.CompilerParams(
            dimension_semantics=("arbitrary", "arbitrary"),
            vmem_limit_bytes=V7X_VMEM_LIMIT_BYTES),
        name="inproj",
    )(x, g, w)


SWA_TS = 512
SWA_TILES = SWA_WIDTH // LANES


def _swa_kernel(q_ref, z_ref, kv_ref, kvh_ref, sink_ref, o_ref):
    i = pl.program_id(1)
    nq = SWA_TILES * SWA_BLOCK
    lane = lax.broadcasted_iota(jnp.int32, (SWA_BLOCK, LANES), 1)
    half_mask = (lane < HEAD_DIM, lane >= HEAD_DIM)
    r = lax.broadcasted_iota(jnp.int32, (2 * SWA_BLOCK, nq), 0)
    c = lax.broadcasted_iota(jnp.int32, (2 * SWA_BLOCK, nq), 1) & (SWA_BLOCK - 1)
    band = (r > c) & (r <= c + SWA_BLOCK)
    for n in range(SWA_TS // SWA_BLOCK):
        rows = slice(n * SWA_BLOCK, (n + 1) * SWA_BLOCK)
        if n == 0:
            kv_prev = kvh_ref[0]
            valid = band & ((r >= SWA_BLOCK) | (i > 0))
        else:
            kv_prev = kv_ref[0, (n - 1) * SWA_BLOCK:n * SWA_BLOCK, :]
            valid = band
        kv2 = jnp.concatenate([kv_prev, kv_ref[0, rows, :]], axis=0)
        k2 = kv2[:, :SWA_KV_WIDTH]
        v2 = kv2[:, SWA_KV_WIDTH:]
        halves = []
        for half in range(SWA_KV_HEADS):
            qs = jnp.concatenate(
                [jnp.where(half_mask[half], q_ref[0, rows, t * LANES:(t + 1) * LANES], 0)
                 for t in range(SWA_TILES)], axis=0)
            st = lax.dot_general(k2, qs, _NT, preferred_element_type=F32)
            st = jnp.where(valid, st, NEG_INF)
            sink = sink_ref[half:half + 1, :]
            m = jnp.maximum(jnp.max(st, axis=0, keepdims=True), sink)
            p = jnp.exp2(st - m)
            denom = jnp.sum(p, axis=0, keepdims=True) + jnp.exp2(sink - m)
            ot = lax.dot_general(v2, p.astype(BF16), _TN, preferred_element_type=F32)
            halves.append(ot[half * HEAD_DIM:(half + 1) * HEAD_DIM] / denom)
        ot = jnp.concatenate(halves, axis=0)
        for t in range(SWA_TILES):
            cols = slice(t * LANES, (t + 1) * LANES)
            o = ot[:, cols].T
            z = z_ref[0, rows, cols].astype(F32)
            o_ref[0, rows, cols] = (o * _silu(z)).astype(BF16)


def _swa(u, sink2):
    b, s, _ = u.shape
    blk = lambda name, w: U_COL[name] // w
    halo_blocks = SWA_TS // SWA_BLOCK
    return pl.pallas_call(
        _swa_kernel,
        grid=(b, s // SWA_TS),
        in_specs=[
            pl.BlockSpec((1, SWA_TS, SWA_WIDTH), lambda bi, i: (bi, i, blk("b_q", SWA_WIDTH))),
            pl.BlockSpec((1, SWA_TS, SWA_WIDTH), lambda bi, i: (bi, i, blk("b_z", SWA_WIDTH))),
            pl.BlockSpec((1, SWA_TS, 2 * SWA_KV_WIDTH),
                         lambda bi, i: (bi, i, blk("b_kv", 2 * SWA_KV_WIDTH))),
            pl.BlockSpec((1, SWA_BLOCK, 2 * SWA_KV_WIDTH),
                         lambda bi, i: (bi, jnp.maximum(i * halo_blocks - 1, 0),
                                        blk("b_kv", 2 * SWA_KV_WIDTH))),
            pl.BlockSpec((SWA_KV_HEADS, SWA_TILES * SWA_BLOCK), lambda bi, i: (0, 0)),
        ],
        out_specs=pl.BlockSpec((1, SWA_TS, SWA_WIDTH), lambda bi, i: (bi, i, 0)),
        out_shape=jax.ShapeDtypeStruct((b, s, SWA_WIDTH), BF16),
        compiler_params=pltpu.CompilerParams(
            dimension_semantics=("arbitrary", "arbitrary"),
            vmem_limit_bytes=V7X_VMEM_LIMIT_BYTES),
        name="swa",
    )(u, u, u, u, sink2)


MOBA_PAIRS = MOBA_WIDTH // LANES
MOBA_SEL_CHUNK = 1024
MOBA_TQ = 512
MOBA_G = MOBA_TQ // MOBA_BLOCK
POS_BIG = 1e30


def _head_split(q):
    lane = lax.broadcasted_iota(jnp.int32, q.shape, 1)
    return (jnp.where(lane < HEAD_DIM, q, 0), jnp.where(lane >= HEAD_DIM, q, 0))


def _moba_select_kernel(q_ref, k_ref, sel_ref, kmean_ref, *, nblk):
    s = k_ref.shape[1]

    def mean_body(j, carry):
        off = pl.multiple_of(j * MOBA_BLOCK, MOBA_BLOCK)
        kj = k_ref[0, pl.ds(off, MOBA_BLOCK), :].astype(F32)
        kmean_ref[pl.ds(j, 1), :] = jnp.sum(kj, axis=0, keepdims=True) * (1.0 / MOBA_BLOCK)
        return carry
    lax.fori_loop(0, nblk, mean_body, 0)

    km = kmean_ref[...]
    km_hi = km.astype(BF16)
    km_lo = (km - km_hi.astype(F32)).astype(BF16)
    ch = MOBA_SEL_CHUNK
    blk_id = lax.broadcasted_iota(jnp.int32, (nblk, ch), 0)
    blk_f = blk_id.astype(F32)
    col = lax.broadcasted_iota(jnp.int32, (nblk, ch), 1)

    def chunk_body(c, carry):
        off = pl.multiple_of(c * ch, ch)
        q_heads = _head_split(q_ref[0, pl.ds(off, ch), :])
        past = blk_id < lax.shift_right_logical(off + col, MOBA_BLOCK.bit_length() - 1)
        for h in range(2):
            g = (lax.dot_general(km_hi, q_heads[h], _NT, preferred_element_type=F32)
                 + lax.dot_general(km_lo, q_heads[h], _NT, preferred_element_type=F32))
            g = jnp.where(past, g, NEG_INF)
            sel = jnp.zeros((nblk, ch), jnp.bool_)
            for _ in range(MOBA_TOPK):
                mx = jnp.max(g, axis=0, keepdims=True)
                first = jnp.min(jnp.where(g == mx, blk_f, float(nblk)), axis=0, keepdims=True)
                pick = blk_f == first
                sel = sel | pick
                g = jnp.where(pick, -jnp.inf, g)
            sel_ref[0, h, :, pl.ds(off, ch)] = jnp.where(sel & past, 1.0, 0.0)
        return carry
    lax.fori_loop(0, s // ch, chunk_body, 0)


def _moba_select(u):
    b, s, _ = u.shape
    nblk = s // MOBA_BLOCK
    blk = lambda name: U_COL[name] // LANES
    return pl.pallas_call(
        functools.partial(_moba_select_kernel, nblk=nblk),
        grid=(b, MOBA_PAIRS),
        in_specs=[pl.BlockSpec((1, s, LANES), lambda bi, p: (bi, 0, blk("c_q") + p)),
                  pl.BlockSpec((1, s, LANES), lambda bi, p: (bi, 0, blk("c_k") + p))],
        out_specs=pl.BlockSpec((1, 2, nblk, s), lambda bi, p: (bi, p, 0, 0)),
        out_shape=jax.ShapeDtypeStruct((b, MOBA_HEADS, nblk, s), F32),
        scratch_shapes=[pltpu.VMEM((nblk, LANES), F32)],
        compiler_params=pltpu.CompilerParams(
            dimension_semantics=("arbitrary", "arbitrary"),
            vmem_limit_bytes=V7X_VMEM_LIMIT_BYTES),
        name="moba_select",
    )(u, u)


def _moba_kernel(q_ref, z_ref, sel_ref, k_ref, v_ref, o_ref):
    t = pl.program_id(2)
    q_heads = _head_split(q_ref[0])
    gk = MOBA_G * MOBA_BLOCK

    def step(k_all, v_all, on, mask, state):
        vt = v_all.T
        new_state = []
        for h in range(2):
            m, l, acc = state[h]
            st = lax.dot_general(k_all, q_heads[h], _NT, preferred_element_type=F32)
            if mask is not None:
                st = jnp.where(mask, st, NEG_INF)
            tiles = [st[g * MOBA_BLOCK:(g + 1) * MOBA_BLOCK] for g in range(MOBA_G)]
            ons = [on[h][g:g + 1] > 0.5 for g in range(MOBA_G)]
            m_new = m
            for tile, o in zip(tiles, ons):
                m_new = jnp.maximum(
                    m_new, jnp.where(o, jnp.max(tile, axis=0, keepdims=True), NEG_INF))
            ps = []
            lsum = jnp.zeros_like(l)
            for tile, o in zip(tiles, ons):
                p = jnp.exp2(tile - jnp.where(o, m_new, POS_BIG))
                lsum = lsum + jnp.sum(p, axis=0, keepdims=True)
                ps.append(p.astype(BF16))
            pv = jnp.dot(vt[h * HEAD_DIM:(h + 1) * HEAD_DIM], jnp.concatenate(ps, axis=0),
                         preferred_element_type=F32)
            alpha = jnp.exp2(m - m_new)
            new_state.append((m_new, alpha * l + lsum, alpha * acc + pv))
        return tuple(new_state)

    def body(j, state):
        off = pl.multiple_of(j * gk, gk)
        on = tuple(jnp.concatenate([sel_ref[0, h, pl.ds(j * MOBA_G + g, 1), :]
                                    for g in range(MOBA_G)], axis=0) for h in range(2))
        return step(k_ref[0, pl.ds(off, gk), :], v_ref[0, pl.ds(off, gk), :], on, None, state)

    init = tuple((jnp.full((1, MOBA_TQ), NEG_INF, F32),
                  jnp.zeros((1, MOBA_TQ), F32),
                  jnp.zeros((HEAD_DIM, MOBA_TQ), F32)) for _ in range(2))
    state = lax.fori_loop(0, t, body, init)

    off = pl.multiple_of(t * gk, gk)
    kr = lax.broadcasted_iota(jnp.int32, (gk, MOBA_TQ), 0)
    qc = lax.broadcasted_iota(jnp.int32, (gk, MOBA_TQ), 1)
    second_half = lax.broadcasted_iota(jnp.int32, (1, MOBA_TQ), 1) >= MOBA_BLOCK
    ones = jnp.ones((1, MOBA_TQ), F32)
    on = tuple(jnp.concatenate(
        [jnp.where(second_half, sel_ref[0, h, pl.ds(t * MOBA_G, 1), :], 1.0), ones], axis=0)
        for h in range(2))
    state = step(k_ref[0, pl.ds(off, gk), :], v_ref[0, pl.ds(off, gk), :], on, kr <= qc, state)

    ot = jnp.concatenate([acc / l for (_, l, acc) in state], axis=0)
    z = z_ref[0].astype(F32)
    o_ref[0] = (ot.T * _silu(z)).astype(BF16)


def _moba(u, sel):
    b, s, _ = u.shape
    nblk = s // MOBA_BLOCK
    blk = lambda name: U_COL[name] // LANES
    return pl.pallas_call(
        _moba_kernel,
        grid=(b, MOBA_PAIRS, s // MOBA_TQ),
        in_specs=[
            pl.BlockSpec((1, MOBA_TQ, LANES), lambda bi, p, t: (bi, t, blk("c_q") + p)),
            pl.BlockSpec((1, MOBA_TQ, LANES), lambda bi, p, t: (bi, t, blk("c_z") + p)),
            pl.BlockSpec((1, 2, nblk, MOBA_TQ), lambda bi, p, t: (bi, p, 0, t)),
            pl.BlockSpec((1, s, LANES), lambda bi, p, t: (bi, 0, blk("c_k") + p)),
            pl.BlockSpec((1, s, LANES), lambda bi, p, t: (bi, 0, blk("c_v") + p)),
        ],
        out_specs=pl.BlockSpec((1, MOBA_TQ, LANES), lambda bi, p, t: (bi, t, p)),
        out_shape=jax.ShapeDtypeStruct((b, s, MOBA_WIDTH), BF16),
        compiler_params=pltpu.CompilerParams(
            dimension_semantics=("arbitrary", "arbitrary", "arbitrary"),
            vmem_limit_bytes=V7X_VMEM_LIMIT_BYTES),
        name="moba",
    )(u, u, sel, u, u)


MERGE_TM = 512


def _merge_kernel(x_ref, ax_ref, axh_ref, az_ref, yb_ref, yc_ref, ga_ref, gb_ref, gc_ref,
                  pw_ref, ps_ref, wa_ref, wb_ref, wc_ref, wo_ref, fg_ref, o_ref, buf_ref,
                  *, final_norm):
    i = pl.program_id(1)
    tm = MERGE_TM
    buf_ref[0:POOL_HALO, :] = jnp.where(i > 0, axh_ref[0].astype(F32), 0.0)
    buf_ref[POOL_HALO:POOL_HALO + tm, :] = ax_ref[0].astype(F32)

    tpos = i * tm + lax.broadcasted_iota(jnp.int32, (tm, POOL_GROUP_DIM), 0)
    ya = []
    for gi, w in enumerate(POOL_WINDOWS):
        cols = slice(gi * POOL_GROUP_DIM, (gi + 1) * POOL_GROUP_DIM)
        cur = buf_ref[POOL_HALO:POOL_HALO + tm, cols]
        win = cur
        for sft in range(1, w):
            win = win + buf_ref[POOL_HALO - sft:POOL_HALO - sft + tm, cols]
        cnt = jnp.minimum(tpos + 1, w).astype(F32)
        pooled = (win / cnt - cur).astype(BF16)
        mixed = jnp.dot(pooled, pw_ref[gi], preferred_element_type=F32) * ps_ref[:, cols]
        ya.append((mixed * _silu(az_ref[0, :, cols].astype(F32))).astype(BF16))
    ya = jnp.concatenate(ya, axis=1)

    merged = _sigmoid(ga_ref[0].astype(F32)) * jnp.dot(ya, wa_ref[...], preferred_element_type=F32)
    merged += _sigmoid(gb_ref[0].astype(F32)) * jnp.dot(yb_ref[0], wb_ref[...],
                                                        preferred_element_type=F32)
    merged += _sigmoid(gc_ref[0].astype(F32)) * jnp.dot(yc_ref[0], wc_ref[...],
                                                        preferred_element_type=F32)
    out = x_ref[0] + jnp.dot(merged.astype(BF16), wo_ref[...], preferred_element_type=F32)
    if final_norm:
        ms = jnp.mean(out * out, axis=-1, keepdims=True)
        out = out * lax.rsqrt(ms + RMS_EPS) * fg_ref[...]
    o_ref[0] = out


def _merge(x, u, yb, yc, pool_w, pool_scale, wa, wb, wc, wo, fg, final_norm):
    b, s, d = x.shape
    tm = MERGE_TM
    ublk = lambda name, w: U_COL[name] // w
    gate_blk = U_COL["gates"] // d
    const = lambda shape: pl.BlockSpec(shape, lambda bi, i: (0,) * len(shape))
    row = lambda w, c: pl.BlockSpec((1, tm, w), lambda bi, i: (bi, i, c))
    return pl.pallas_call(
        functools.partial(_merge_kernel, final_norm=final_norm),
        grid=(b, s // tm),
        in_specs=[
            row(d, 0),
            row(POOL_WIDTH, ublk("a_x", POOL_WIDTH)),
            pl.BlockSpec((1, POOL_HALO, POOL_WIDTH),
                         lambda bi, i: (bi, jnp.maximum(i * (tm // POOL_HALO) - 1, 0),
                                        ublk("a_x", POOL_WIDTH))),
            row(POOL_WIDTH, ublk("a_z", POOL_WIDTH)),
            row(SWA_WIDTH, 0),
            row(MOBA_WIDTH, 0),
            row(d, gate_blk), row(d, gate_blk + 1), row(d, gate_blk + 2),
            const((POOL_GROUPS, POOL_GROUP_DIM, POOL_GROUP_DIM)),
            const((1, POOL_WIDTH)),
            const((POOL_WIDTH, d)), const((SWA_WIDTH, d)), const((MOBA_WIDTH, d)),
            const((d, d)),
            const((1, d)),
        ],
        out_specs=row(d, 0),
        out_shape=jax.ShapeDtypeStruct((b, s, d), F32),
        scratch_shapes=[pltpu.VMEM((POOL_HALO + tm, POOL_WIDTH), F32)],
        compiler_params=pltpu.CompilerParams(
            dimension_semantics=("arbitrary", "arbitrary"),
            vmem_limit_bytes=V7X_VMEM_LIMIT_BYTES),
        name="merge",
    )(x, u, u, u, yb, yc, u, u, u, pool_w, pool_scale, wa, wb, wc, wo, fg)


def kernel(x, norm_g, w_in, pool_w, pool_scale, sink_logits, w_proj_a, w_proj_b, w_proj_c,
           w_out, final_norm_g):
    depth = w_in.shape[0]
    assert x.shape[1] % max(INPROJ_TM, SWA_TS, MOBA_TQ, MOBA_SEL_CHUNK, MERGE_TM) == 0
    assert x.shape[2] == D_MODEL and w_in.shape[2] == sum(_REF_SIZES)
    head_order = np.asarray(_SWA_HEAD_ORDER)
    swa_rows = (head_order[:, None] * HEAD_DIM + np.arange(HEAD_DIM)[None, :]).reshape(-1)
    fg = final_norm_g.reshape(1, D_MODEL)
    for l in range(depth):
        w_u = (w_in[l][:, _U_COLS] * _U_SCALE[None, :]).astype(BF16)
        sink2 = (sink_logits[l].astype(F32) * LOG2E)[
            np.arange(SWA_TILES)[None, :] + SWA_TILES * np.arange(SWA_KV_HEADS)[:, None]]
        sink2 = jnp.repeat(sink2, SWA_BLOCK, axis=1)
        u = _inproj(x, norm_g[l].reshape(1, D_MODEL), w_u)
        yb = _swa(u, sink2)
        yc = _moba(u, _moba_select(u))
        x = _merge(x, u, yb, yc,
                   pool_w[l].astype(BF16), pool_scale[l].reshape(1, POOL_WIDTH),
                   w_proj_a[l].astype(BF16), w_proj_b[l][swa_rows].astype(BF16),
                   w_proj_c[l].astype(BF16), w_out[l].astype(BF16), fg,
                   final_norm=(l == depth - 1))
    return x
```

```python
import functools

import numpy as np
import jax
import jax.numpy as jnp
from jax import lax
from jax.experimental import pallas as pl
from jax.experimental.pallas import tpu as pltpu

F32 = jnp.float32
BF16 = jnp.bfloat16

LANES = 128
V7X_VMEM_LIMIT_BYTES = 56 * 1024 * 1024

D_MODEL = 1024
HEAD_DIM = 64
RMS_EPS = 1e-6
NEG_INF = -1e30
LOG2E = 1.4426950408889634
ATTN_SCALE = HEAD_DIM ** -0.5

POOL_WIDTH = 512
POOL_GROUPS = 4
POOL_GROUP_DIM = POOL_WIDTH // POOL_GROUPS
POOL_WINDOWS = (2, 4, 8, 16)
POOL_HALO = 16

SWA_Q_HEADS = 8
SWA_KV_HEADS = 2
SWA_BLOCK = 128
SWA_WIDTH = SWA_Q_HEADS * HEAD_DIM
SWA_KV_WIDTH = SWA_KV_HEADS * HEAD_DIM

MOBA_HEADS = 8
MOBA_BLOCK = 256
MOBA_TOPK = 3
MOBA_WIDTH = MOBA_HEADS * HEAD_DIM

N_BRANCH = 3

_REF_SIZES = (POOL_WIDTH, POOL_WIDTH, SWA_WIDTH, SWA_KV_WIDTH, SWA_KV_WIDTH, SWA_WIDTH,
              MOBA_WIDTH, MOBA_WIDTH, MOBA_WIDTH, MOBA_WIDTH, N_BRANCH * D_MODEL)
_REF_OFF = dict(zip(("a_x", "a_z", "b_q", "b_k", "b_v", "b_z", "c_q", "c_k", "c_v", "c_z", "gates"),
                    np.cumsum((0,) + _REF_SIZES[:-1]).tolist()))

_SWA_HEAD_ORDER = (0, 4, 1, 5, 2, 6, 3, 7)

U_COL = dict(a_x=0, a_z=512, b_q=1024, b_z=1536, c_q=2048, c_k=2560, c_v=3072, c_z=3584,
             gates=4096, b_kv=7168)
U_WIDTH = 7168 + 2 * SWA_KV_WIDTH


def _u_permutation():
    def rng(name, n):
        return np.arange(_REF_OFF[name], _REF_OFF[name] + n)

    def swa_heads(name):
        base = _REF_OFF[name]
        return np.concatenate([np.arange(base + h * HEAD_DIM, base + (h + 1) * HEAD_DIM)
                               for h in _SWA_HEAD_ORDER])

    cols = np.concatenate([
        rng("a_x", POOL_WIDTH), rng("a_z", POOL_WIDTH),
        swa_heads("b_q"), swa_heads("b_z"),
        rng("c_q", MOBA_WIDTH), rng("c_k", MOBA_WIDTH), rng("c_v", MOBA_WIDTH), rng("c_z", MOBA_WIDTH),
        rng("gates", N_BRANCH * D_MODEL),
        rng("b_k", SWA_KV_WIDTH), rng("b_v", SWA_KV_WIDTH)])
    scale = np.ones((U_WIDTH,), np.float32)
    qscale = np.float32(ATTN_SCALE * LOG2E)
    scale[U_COL["b_q"]:U_COL["b_q"] + SWA_WIDTH] = qscale
    scale[U_COL["c_q"]:U_COL["c_q"] + MOBA_WIDTH] = qscale
    assert cols.shape == (U_WIDTH,)
    return cols, scale


_U_COLS, _U_SCALE = _u_permutation()


def _sigmoid(z):
    return 0.5 * jnp.tanh(0.5 * z) + 0.5


def _silu(z):
    return z * _sigmoid(z)


_NT = (((1,), (1,)), ((), ()))
_TN = (((0,), (0,)), ((), ()))


INPROJ_TM = 512
INPROJ_CHUNK = 1024


def _inproj_kernel(x_ref, g_ref, w_ref, u_ref):
    x = x_ref[0]
    ms = jnp.mean(x * x, axis=-1, keepdims=True)
    h = (x * lax.rsqrt(ms + RMS_EPS) * g_ref[...]).astype(BF16)
    for c0 in range(0, U_WIDTH, INPROJ_CHUNK):
        c1 = min(c0 + INPROJ_CHUNK, U_WIDTH)
        u_ref[0, :, c0:c1] = jnp.dot(h, w_ref[:, c0:c1],
                                     preferred_element_type=F32).astype(BF16)


def _inproj(x, g, w):
    b, s, d = x.shape
    return pl.pallas_call(
        _inproj_kernel,
        grid=(b, s // INPROJ_TM),
        in_specs=[pl.BlockSpec((1, INPROJ_TM, d), lambda bi, i: (bi, i, 0)),
                  pl.BlockSpec((1, d), lambda bi, i: (0, 0)),
                  pl.BlockSpec((d, U_WIDTH), lambda bi, i: (0, 0))],
        out_specs=pl.BlockSpec((1, INPROJ_TM, U_WIDTH), lambda bi, i: (bi, i, 0)),
        out_shape=jax.ShapeDtypeStruct((b, s, U_WIDTH), BF16),
        compiler_params=pltpu.CompilerParams(
            dimension_semantics=("arbitrary", "arbitrary"),
            vmem_limit_bytes=V7X_VMEM_LIMIT_BYTES),
        name="inproj",
    )(x, g, w)


SWA_TS = 512
SWA_TILES = SWA_WIDTH // LANES


def _swa_kernel(q_ref, z_ref, kv_ref, kvh_ref, sink_ref, o_ref):
    i = pl.program_id(1)
    nq = SWA_TILES * SWA_BLOCK
    lane = lax.broadcasted_iota(jnp.int32, (SWA_BLOCK, LANES), 1)
    half_mask = (lane < HEAD_DIM, lane >= HEAD_DIM)
    r = lax.broadcasted_iota(jnp.int32, (2 * SWA_BLOCK, nq), 0)
    c = lax.broadcasted_iota(jnp.int32, (2 * SWA_BLOCK, nq), 1) & (SWA_BLOCK - 1)
    band = (r > c) & (r <= c + SWA_BLOCK)
    for n in range(SWA_TS // SWA_BLOCK):
        rows = slice(n * SWA_BLOCK, (n + 1) * SWA_BLOCK)
        if n == 0:
            kv_prev = kvh_ref[0]
            valid = band & ((r >= SWA_BLOCK) | (i > 0))
        else:
            kv_prev = kv_ref[0, (n - 1) * SWA_BLOCK:n * SWA_BLOCK, :]
            valid = band
        kv2 = jnp.concatenate([kv_prev, kv_ref[0, rows, :]], axis=0)
        k2 = kv2[:, :SWA_KV_WIDTH]
        v2 = kv2[:, SWA_KV_WIDTH:]
        halves = []
        for half in range(SWA_KV_HEADS):
            qs = jnp.concatenate(
                [jnp.where(half_mask[half], q_ref[0, rows, t * LANES:(t + 1) * LANES], 0)
                 for t in range(SWA_TILES)], axis=0)
            st = lax.dot_general(k2, qs, _NT, preferred_element_type=F32)
            st = jnp.where(valid, st, NEG_INF)
            sink = sink_ref[half:half + 1, :]
            m = jnp.maximum(jnp.max(st, axis=0, keepdims=True), sink)
            p = jnp.exp2(st - m)
            denom = jnp.sum(p, axis=0, keepdims=True) + jnp.exp2(sink - m)
            ot = lax.dot_general(v2, p.astype(BF16), _TN, preferred_element_type=F32)
            halves.append(ot[half * HEAD_DIM:(half + 1) * HEAD_DIM] / denom)
        ot = jnp.concatenate(halves, axis=0)
        for t in range(SWA_TILES):
            cols = slice(t * LANES, (t + 1) * LANES)
            o = ot[:, cols].T
            z = z_ref[0, rows, cols].astype(F32)
            o_ref[0, rows, cols] = (o * _silu(z)).astype(BF16)


def _swa(u, sink2):
    b, s, _ = u.shape
    blk = lambda name, w: U_COL[name] // w
    halo_blocks = SWA_TS // SWA_BLOCK
    return pl.pallas_call(
        _swa_kernel,
        grid=(b, s // SWA_TS),
        in_specs=[
            pl.BlockSpec((1, SWA_TS, SWA_WIDTH), lambda bi, i: (bi, i, blk("b_q", SWA_WIDTH))),
            pl.BlockSpec((1, SWA_TS, SWA_WIDTH), lambda bi, i: (bi, i, blk("b_z", SWA_WIDTH))),
            pl.BlockSpec((1, SWA_TS, 2 * SWA_KV_WIDTH),
                         lambda bi, i: (bi, i, blk("b_kv", 2 * SWA_KV_WIDTH))),
            pl.BlockSpec((1, SWA_BLOCK, 2 * SWA_KV_WIDTH),
                         lambda bi, i: (bi, jnp.maximum(i * halo_blocks - 1, 0),
                                        blk("b_kv", 2 * SWA_KV_WIDTH))),
            pl.BlockSpec((SWA_KV_HEADS, SWA_TILES * SWA_BLOCK), lambda bi, i: (0, 0)),
        ],
        out_specs=pl.BlockSpec((1, SWA_TS, SWA_WIDTH), lambda bi, i: (bi, i, 0)),
        out_shape=jax.ShapeDtypeStruct((b, s, SWA_WIDTH), BF16),
        compiler_params=pltpu.CompilerParams(
            dimension_semantics=("arbitrary", "arbitrary"),
            vmem_limit_bytes=V7X_VMEM_LIMIT_BYTES),
        name="swa",
    )(u, u, u, u, sink2)


MOBA_PAIRS = MOBA_WIDTH // LANES
MOBA_SEL_CHUNK = 1024
MOBA_TQ = 512
MOBA_G = MOBA_TQ // MOBA_BLOCK
POS_BIG = 1e30


def _head_split(q):
    lane = lax.broadcasted_iota(jnp.int32, q.shape, 1)
    return (jnp.where(lane < HEAD_DIM, q, 0), jnp.where(lane >= HEAD_DIM, q, 0))


def _moba_select_kernel(q_ref, k_ref, sel_ref, kmean_ref, *, nblk):
    s = k_ref.shape[1]

    def mean_body(j, carry):
        off = pl.multiple_of(j * MOBA_BLOCK, MOBA_BLOCK)
        kj = k_ref[0, pl.ds(off, MOBA_BLOCK), :].astype(F32)
        kmean_ref[pl.ds(j, 1), :] = jnp.sum(kj, axis=0, keepdims=True) * (1.0 / MOBA_BLOCK)
        return carry
    lax.fori_loop(0, nblk, mean_body, 0)

    km = kmean_ref[...]
    km_hi = km.astype(BF16)
    km_lo = (km - km_hi.astype(F32)).astype(BF16)
    ch = MOBA_SEL_CHUNK
    blk_id = lax.broadcasted_iota(jnp.int32, (nblk, ch), 0)
    blk_f = blk_id.astype(F32)
    col = lax.broadcasted_iota(jnp.int32, (nblk, ch), 1)

    def chunk_body(c, carry):
        off = pl.multiple_of(c * ch, ch)
        q_heads = _head_split(q_ref[0, pl.ds(off, ch), :])
        past = blk_id < lax.shift_right_logical(off + col, MOBA_BLOCK.bit_length() - 1)
        for h in range(2):
            g = (lax.dot_general(km_hi, q_heads[h], _NT, preferred_element_type=F32)
                 + lax.dot_general(km_lo, q_heads[h], _NT, preferred_element_type=F32))
            g = jnp.where(past, g, NEG_INF)
            sel = jnp.zeros((nblk, ch), jnp.bool_)
            for _ in range(MOBA_TOPK):
                mx = jnp.max(g, axis=0, keepdims=True)
                first = jnp.min(jnp.where(g == mx, blk_f, float(nblk)), axis=0, keepdims=True)
                pick = blk_f == first
                sel = sel | pick
                g = jnp.where(pick, -jnp.inf, g)
            sel_ref[0, h, :, pl.ds(off, ch)] = jnp.where(sel & past, 1.0, 0.0)
        return carry
    lax.fori_loop(0, s // ch, chunk_body, 0)


def _moba_select(u):
    b, s, _ = u.shape
    nblk = s // MOBA_BLOCK
    blk = lambda name: U_COL[name] // LANES
    return pl.pallas_call(
        functools.partial(_moba_select_kernel, nblk=nblk),
        grid=(b, MOBA_PAIRS),
        in_specs=[pl.BlockSpec((1, s, LANES), lambda bi, p: (bi, 0, blk("c_q") + p)),
                  pl.BlockSpec((1, s, LANES), lambda bi, p: (bi, 0, blk("c_k") + p))],
        out_specs=pl.BlockSpec((1, 2, nblk, s), lambda bi, p: (bi, p, 0, 0)),
        out_shape=jax.ShapeDtypeStruct((b, MOBA_HEADS, nblk, s), F32),
        scratch_shapes=[pltpu.VMEM((nblk, LANES), F32)],
        compiler_params=pltpu.CompilerParams(
            dimension_semantics=("arbitrary", "arbitrary"),
            vmem_limit_bytes=V7X_VMEM_LIMIT_BYTES),
        name="moba_select",
    )(u, u)


SUM_ROWS = 16


def _moba_kernel(q_ref, z_ref, sel_ref, k_ref, v_ref, o_ref, sa_ref, sb_ref):
    t = pl.program_id(2)
    q_heads = _head_split(q_ref[0])
    gk = MOBA_G * MOBA_BLOCK
    blocks = [slice(b * MOBA_BLOCK, (b + 1) * MOBA_BLOCK) for b in range(MOBA_G)]
    sum_rows = jnp.ones((SUM_ROWS, MOBA_BLOCK), BF16)

    def scores(g, h, rows, sc_ref):
        off = pl.multiple_of(g * gk, gk)
        st = lax.dot_general(k_ref[0, pl.ds(off, gk), :][rows], q_heads[h], _NT,
                             preferred_element_type=F32)
        sc_ref[h, rows, :] = st
        return jnp.max(st, axis=0, keepdims=True)

    def produce(g, sc_ref):
        return [[scores(g, h, rows, sc_ref) for rows in blocks] for h in range(2)]

    def step(g, sc_ref, maxima, on, mask, state, nxt):
        off = pl.multiple_of(g * gk, gk)
        vt = v_ref[0, pl.ds(off, gk), :].T
        new_state, next_maxima = [], []
        for h in range(2):
            m, l, acc = state[h]
            tiles = [sc_ref[h, rows, :] for rows in blocks]
            if mask is not None:
                tiles = [jnp.where(mask[rows], tile, NEG_INF) for tile, rows in zip(tiles, blocks)]
                block_max = [jnp.max(tile, axis=0, keepdims=True) for tile in tiles]
            else:
                block_max = maxima[h]
            ons = [o > 0.5 for o in on[h]]
            m_new = m
            for mb, o in zip(block_max, ons):
                m_new = jnp.maximum(m_new, jnp.where(o, mb, NEG_INF))
            pv = None
            head_maxima = []
            for tile, o, rows in zip(tiles, ons, blocks):
                p = jnp.exp2(tile - jnp.where(o, m_new, POS_BIG)).astype(BF16)
                vt_b = jnp.concatenate([vt[h * HEAD_DIM:(h + 1) * HEAD_DIM, rows], sum_rows], axis=0)
                part = jnp.dot(vt_b, p, preferred_element_type=F32)
                pv = part if pv is None else pv + part
                if nxt is not None:
                    head_maxima.append(scores(nxt[0], h, rows, nxt[1]))
            alpha = jnp.exp2(m - m_new)
            new_state.append((m_new, alpha * l + pv[HEAD_DIM:HEAD_DIM + 1],
                              alpha * acc + pv[:HEAD_DIM]))
            next_maxima.append(head_maxima)
        return tuple(new_state), next_maxima

    odd = lax.rem(t, 2)

    def half_step(s, cur_ref, nxt_ref, state, maxima):
        g = s - odd
        real = (g >= 0).astype(F32)
        gc = jnp.maximum(g, 0)
        on = [[sel_ref[0, h, pl.ds(gc * MOBA_G + b, 1), :] * real for b in range(MOBA_G)]
              for h in range(2)]
        return step(gc, cur_ref, maxima, on, None, state, (s + 1 - odd, nxt_ref))

    def body(jj, carry):
        state, maxima = carry
        state, maxima = half_step(2 * jj, sa_ref, sb_ref, state, maxima)
        return half_step(2 * jj + 1, sb_ref, sa_ref, state, maxima)

    init = tuple((jnp.full((1, MOBA_TQ), NEG_INF, F32),
                  jnp.zeros((1, MOBA_TQ), F32),
                  jnp.zeros((HEAD_DIM, MOBA_TQ), F32)) for _ in range(2))
    state, _ = lax.fori_loop(0, (t + odd) // 2, body, (init, produce(0, sa_ref)))

    kr = lax.broadcasted_iota(jnp.int32, (gk, MOBA_TQ), 0)
    qc = lax.broadcasted_iota(jnp.int32, (gk, MOBA_TQ), 1)
    second_half = lax.broadcasted_iota(jnp.int32, (1, MOBA_TQ), 1) >= MOBA_BLOCK
    ones = jnp.ones((1, MOBA_TQ), F32)
    on = [[jnp.where(second_half, sel_ref[0, h, pl.ds(t * MOBA_G, 1), :], 1.0), ones]
          for h in range(2)]
    state, _ = step(t, sa_ref, None, on, kr <= qc, state, None)


    ot = jnp.concatenate([acc / l for (_, l, acc) in state], axis=0)
    z = z_ref[0].astype(F32)
    o_ref[0] = (ot.T * _silu(z)).astype(BF16)


def _moba(u, sel):
    b, s, _ = u.shape
    nblk = s // MOBA_BLOCK
    blk = lambda name: U_COL[name] // LANES
    return pl.pallas_call(
        _moba_kernel,
        grid=(b, MOBA_PAIRS, s // MOBA_TQ),
        in_specs=[
            pl.BlockSpec((1, MOBA_TQ, LANES), lambda bi, p, t: (bi, t, blk("c_q") + p)),
            pl.BlockSpec((1, MOBA_TQ, LANES), lambda bi, p, t: (bi, t, blk("c_z") + p)),
            pl.BlockSpec((1, 2, nblk, MOBA_TQ), lambda bi, p, t: (bi, p, 0, t)),
            pl.BlockSpec((1, s, LANES), lambda bi, p, t: (bi, 0, blk("c_k") + p)),
            pl.BlockSpec((1, s, LANES), lambda bi, p, t: (bi, 0, blk("c_v") + p)),
        ],
        out_specs=pl.BlockSpec((1, MOBA_TQ, LANES), lambda bi, p, t: (bi, t, p)),
        out_shape=jax.ShapeDtypeStruct((b, s, MOBA_WIDTH), BF16),
        scratch_shapes=[pltpu.VMEM((2, MOBA_G * MOBA_BLOCK, MOBA_TQ), F32)] * 2,
        compiler_params=pltpu.CompilerParams(
            dimension_semantics=("arbitrary", "arbitrary", "arbitrary"),
            vmem_limit_bytes=V7X_VMEM_LIMIT_BYTES),
        name="moba",
    )(u, u, sel, u, u)


MERGE_TM = 512


def _merge_kernel(x_ref, ax_ref, axh_ref, az_ref, yb_ref, yc_ref, ga_ref, gb_ref, gc_ref,
                  pw_ref, ps_ref, wa_ref, wb_ref, wc_ref, wo_ref, fg_ref, o_ref, buf_ref,
                  *, final_norm):
    i = pl.program_id(1)
    tm = MERGE_TM
    buf_ref[0:POOL_HALO, :] = jnp.where(i > 0, axh_ref[0].astype(F32), 0.0)
    buf_ref[POOL_HALO:POOL_HALO + tm, :] = ax_ref[0].astype(F32)

    tpos = i * tm + lax.broadcasted_iota(jnp.int32, (tm, POOL_GROUP_DIM), 0)
    ya = []
    for gi, w in enumerate(POOL_WINDOWS):
        cols = slice(gi * POOL_GROUP_DIM, (gi + 1) * POOL_GROUP_DIM)
        cur = buf_ref[POOL_HALO:POOL_HALO + tm, cols]
        win = cur
        for sft in range(1, w):
            win = win + buf_ref[POOL_HALO - sft:POOL_HALO - sft + tm, cols]
        cnt = jnp.minimum(tpos + 1, w).astype(F32)
        pooled = (win / cnt - cur).astype(BF16)
        mixed = jnp.dot(pooled, pw_ref[gi], preferred_element_type=F32) * ps_ref[:, cols]
        ya.append((mixed * _silu(az_ref[0, :, cols].astype(F32))).astype(BF16))
    ya = jnp.concatenate(ya, axis=1)

    merged = _sigmoid(ga_ref[0].astype(F32)) * jnp.dot(ya, wa_ref[...], preferred_element_type=F32)
    merged += _sigmoid(gb_ref[0].astype(F32)) * jnp.dot(yb_ref[0], wb_ref[...],
                                                        preferred_element_type=F32)
    merged += _sigmoid(gc_ref[0].astype(F32)) * jnp.dot(yc_ref[0], wc_ref[...],
                                                        preferred_element_type=F32)
    out = x_ref[0] + jnp.dot(merged.astype(BF16), wo_ref[...], preferred_element_type=F32)
    if final_norm:
        ms = jnp.mean(out * out, axis=-1, keepdims=True)
        out = out * lax.rsqrt(ms + RMS_EPS) * fg_ref[...]
    o_ref[0] = out


def _merge(x, u, yb, yc, pool_w, pool_scale, wa, wb, wc, wo, fg, final_norm):
    b, s, d = x.shape
    tm = MERGE_TM
    ublk = lambda name, w: U_COL[name] // w
    gate_blk = U_COL["gates"] // d
    const = lambda shape: pl.BlockSpec(shape, lambda bi, i: (0,) * len(shape))
    row = lambda w, c: pl.BlockSpec((1, tm, w), lambda bi, i: (bi, i, c))
    return pl.pallas_call(
        functools.partial(_merge_kernel, final_norm=final_norm),
        grid=(b, s // tm),
        in_specs=[
            row(d, 0),
            row(POOL_WIDTH, ublk("a_x", POOL_WIDTH)),
            pl.BlockSpec((1, POOL_HALO, POOL_WIDTH),
                         lambda bi, i: (bi, jnp.maximum(i * (tm // POOL_HALO) - 1, 0),
                                        ublk("a_x", POOL_WIDTH))),
            row(POOL_WIDTH, ublk("a_z", POOL_WIDTH)),
            row(SWA_WIDTH, 0),
            row(MOBA_WIDTH, 0),
            row(d, gate_blk), row(d, gate_blk + 1), row(d, gate_blk + 2),
            const((POOL_GROUPS, POOL_GROUP_DIM, POOL_GROUP_DIM)),
            const((1, POOL_WIDTH)),
            const((POOL_WIDTH, d)), const((SWA_WIDTH, d)), const((MOBA_WIDTH, d)),
            const((d, d)),
            const((1, d)),
        ],
        out_specs=row(d, 0),
        out_shape=jax.ShapeDtypeStruct((b, s, d), F32),
        scratch_shapes=[pltpu.VMEM((POOL_HALO + tm, POOL_WIDTH), F32)],
        compiler_params=pltpu.CompilerParams(
            dimension_semantics=("arbitrary", "arbitrary"),
            vmem_limit_bytes=V7X_VMEM_LIMIT_BYTES),
        name="merge",
    )(x, u, u, u, yb, yc, u, u, u, pool_w, pool_scale, wa, wb, wc, wo, fg)


def kernel(x, norm_g, w_in, pool_w, pool_scale, sink_logits, w_proj_a, w_proj_b, w_proj_c,
           w_out, final_norm_g):
    depth = w_in.shape[0]
    assert x.shape[1] % max(INPROJ_TM, SWA_TS, MOBA_TQ, MOBA_SEL_CHUNK, MERGE_TM) == 0
    assert x.shape[2] == D_MODEL and w_in.shape[2] == sum(_REF_SIZES)
    head_order = np.asarray(_SWA_HEAD_ORDER)
    swa_rows = (head_order[:, None] * HEAD_DIM + np.arange(HEAD_DIM)[None, :]).reshape(-1)
    fg = final_norm_g.reshape(1, D_MODEL)
    for l in range(depth):
        w_u = (w_in[l][:, _U_COLS] * _U_SCALE[None, :]).astype(BF16)
        sink2 = (sink_logits[l].astype(F32) * LOG2E)[
            np.arange(SWA_TILES)[None, :] + SWA_TILES * np.arange(SWA_KV_HEADS)[:, None]]
        sink2 = jnp.repeat(sink2, SWA_BLOCK, axis=1)
        u = _inproj(x, norm_g[l].reshape(1, D_MODEL), w_u)
        yb = _swa(u, sink2)
        yc = _moba(u, _moba_select(u))
        x = _merge(x, u, yb, yc,
                   pool_w[l].astype(BF16), pool_scale[l].reshape(1, POOL_WIDTH),
                   w_proj_a[l].astype(BF16), w_proj_b[l][swa_rows].astype(BF16),
                   w_proj_c[l].astype(BF16), w_out[l].astype(BF16), fg,
                   final_norm=(l == depth - 1))
    return x
```

```python
import functools

import numpy as np
import jax
import jax.numpy as jnp
from jax import lax
from jax.experimental import pallas as pl
from jax.experimental.pallas import tpu as pltpu

F32 = jnp.float32
BF16 = jnp.bfloat16

LANES = 128
V7X_VMEM_LIMIT_BYTES = 56 * 1024 * 1024

D_MODEL = 1024
HEAD_DIM = 64
RMS_EPS = 1e-6
NEG_INF = -1e30
LOG2E = 1.4426950408889634
ATTN_SCALE = HEAD_DIM ** -0.5

POOL_WIDTH = 512
POOL_GROUPS = 4
POOL_GROUP_DIM = POOL_WIDTH // POOL_GROUPS
POOL_WINDOWS = (2, 4, 8, 16)
POOL_HALO = 16

SWA_Q_HEADS = 8
SWA_KV_HEADS = 2
SWA_BLOCK = 128
SWA_WIDTH = SWA_Q_HEADS * HEAD_DIM
SWA_KV_WIDTH = SWA_KV_HEADS * HEAD_DIM

MOBA_HEADS = 8
MOBA_BLOCK = 256
MOBA_TOPK = 3
MOBA_WIDTH = MOBA_HEADS * HEAD_DIM

N_BRANCH = 3

_REF_SIZES = (POOL_WIDTH, POOL_WIDTH, SWA_WIDTH, SWA_KV_WIDTH, SWA_KV_WIDTH, SWA_WIDTH,
              MOBA_WIDTH, MOBA_WIDTH, MOBA_WIDTH, MOBA_WIDTH, N_BRANCH * D_MODEL)
_REF_OFF = dict(zip(("a_x", "a_z", "b_q", "b_k", "b_v", "b_z", "c_q", "c_k", "c_v", "c_z", "gates"),
                    np.cumsum((0,) + _REF_SIZES[:-1]).tolist()))

_SWA_HEAD_ORDER = (0, 4, 1, 5, 2, 6, 3, 7)

U_COL = dict(a_x=0, a_z=512, b_q=1024, b_z=1536, c_q=2048, c_k=2560, c_v=3072, c_z=3584,
             gates=4096, b_kv=7168)
U_WIDTH = 7168 + 2 * SWA_KV_WIDTH


GATE_SCALE = 0.5


def _u_segments():
    qscale = ATTN_SCALE * LOG2E

    def swa_heads(name, scale):
        return [(_REF_OFF[name] + h * HEAD_DIM, HEAD_DIM, scale) for h in _SWA_HEAD_ORDER]

    segs = ([(_REF_OFF["a_x"], 2 * POOL_WIDTH, 1.0)]
            + swa_heads("b_q", qscale) + swa_heads("b_z", 1.0)
            + [(_REF_OFF["c_q"], MOBA_WIDTH, qscale),
               (_REF_OFF["c_k"], 3 * MOBA_WIDTH, 1.0),
               (_REF_OFF["gates"], N_BRANCH * D_MODEL, GATE_SCALE),
               (_REF_OFF["b_k"], 2 * SWA_KV_WIDTH, 1.0)])
    assert sum(w for _, w, _ in segs) == U_WIDTH
    return segs


_U_SEGMENTS = _u_segments()


def _u_weights(w):
    parts = [w[:, c0:c0 + n] if scale == 1.0 else w[:, c0:c0 + n] * np.float32(scale)
             for c0, n, scale in _U_SEGMENTS]
    return jnp.concatenate(parts, axis=1).astype(BF16)


def _sigmoid(z):
    return 0.5 * jnp.tanh(0.5 * z) + 0.5


def _silu(z):
    return z * _sigmoid(z)


_NT = (((1,), (1,)), ((), ()))
_TN = (((0,), (0,)), ((), ()))


INPROJ_TM = 512
INPROJ_CHUNK = 1024


def _inproj_kernel(x_ref, g_ref, w_ref, u_ref):
    x = x_ref[0]
    ms = jnp.mean(x * x, axis=-1, keepdims=True)
    h = (x * lax.rsqrt(ms + RMS_EPS) * g_ref[...]).astype(BF16)
    for c0 in range(0, U_WIDTH, INPROJ_CHUNK):
        c1 = min(c0 + INPROJ_CHUNK, U_WIDTH)
        u_ref[0, :, c0:c1] = jnp.dot(h, w_ref[:, c0:c1],
                                     preferred_element_type=F32).astype(BF16)


def _inproj(x, g, w):
    b, s, d = x.shape
    return pl.pallas_call(
        _inproj_kernel,
        grid=(b, s // INPROJ_TM),
        in_specs=[pl.BlockSpec((1, INPROJ_TM, d), lambda bi, i: (bi, i, 0)),
                  pl.BlockSpec((1, d), lambda bi, i: (0, 0)),
                  pl.BlockSpec((d, U_WIDTH), lambda bi, i: (0, 0))],
        out_specs=pl.BlockSpec((1, INPROJ_TM, U_WIDTH), lambda bi, i: (bi, i, 0)),
        out_shape=jax.ShapeDtypeStruct((b, s, U_WIDTH), BF16),
        compiler_params=pltpu.CompilerParams(
            dimension_semantics=("arbitrary", "arbitrary"),
            vmem_limit_bytes=V7X_VMEM_LIMIT_BYTES),
        name="inproj",
    )(x, g, w)


SWA_TS = 512
SWA_TILES = SWA_WIDTH // LANES
SWA_LOOKAHEAD = 2
SUM_ROWS = 16


def _swa_kernel(q_ref, z_ref, kv_ref, kvh_ref, sink_ref, o_ref):
    i = pl.program_id(1)
    nq = SWA_TILES * SWA_BLOCK
    lane = lax.broadcasted_iota(jnp.int32, (SWA_BLOCK, LANES), 1)
    half_mask = (lane < HEAD_DIM, lane >= HEAD_DIM)
    r = lax.broadcasted_iota(jnp.int32, (2 * SWA_BLOCK, nq), 0)
    c = lax.broadcasted_iota(jnp.int32, (2 * SWA_BLOCK, nq), 1) & (SWA_BLOCK - 1)
    band = (r > c) & (r <= c + SWA_BLOCK)
    sum_rows = jnp.ones((SUM_ROWS, 2 * SWA_BLOCK), BF16)

    def kv_pair(n):
        kv_prev = kvh_ref[0] if n == 0 else kv_ref[0, (n - 1) * SWA_BLOCK:n * SWA_BLOCK, :]
        return jnp.concatenate([kv_prev, kv_ref[0, n * SWA_BLOCK:(n + 1) * SWA_BLOCK, :]], axis=0)

    def qk(n, half):
        rows = slice(n * SWA_BLOCK, (n + 1) * SWA_BLOCK)
        qs = jnp.concatenate(
            [jnp.where(half_mask[half], q_ref[0, rows, t * LANES:(t + 1) * LANES], 0)
             for t in range(SWA_TILES)], axis=0)
        return lax.dot_general(kv_pair(n)[:, :SWA_KV_WIDTH], qs, _NT,
                               preferred_element_type=F32)

    def softmax_pv(n, half, st):
        valid = band & ((r >= SWA_BLOCK) | (i > 0)) if n == 0 else band
        st = jnp.where(valid, st, NEG_INF)
        sink = sink_ref[half:half + 1, :]
        m = jnp.maximum(jnp.max(st, axis=0, keepdims=True), sink)
        p = jnp.exp2(st - m).astype(BF16)
        vt = kv_pair(n)[:, SWA_KV_WIDTH:].T
        vt_h = jnp.concatenate([vt[half * HEAD_DIM:(half + 1) * HEAD_DIM], sum_rows], axis=0)
        pv = jnp.dot(vt_h, p, preferred_element_type=F32)
        return pv[:HEAD_DIM] / (pv[HEAD_DIM:HEAD_DIM + 1] + jnp.exp2(sink - m))

    def finish(n, halves):
        rows = slice(n * SWA_BLOCK, (n + 1) * SWA_BLOCK)
        ot = jnp.concatenate(halves, axis=0)
        for t in range(SWA_TILES):
            cols = slice(t * LANES, (t + 1) * LANES)
            o = ot[:, cols].T
            z = z_ref[0, rows, cols].astype(F32)
            o_ref[0, rows, cols] = (o * _silu(z)).astype(BF16)

    tiles = [(n, half) for n in range(SWA_TS // SWA_BLOCK) for half in range(SWA_KV_HEADS)]
    pending, halves = {}, []
    for idx in range(len(tiles) + SWA_LOOKAHEAD):
        if idx < len(tiles):
            pending[idx] = qk(*tiles[idx])
        done = idx - SWA_LOOKAHEAD
        if done >= 0:
            n, half = tiles[done]
            halves.append(softmax_pv(n, half, pending.pop(done)))
            if half == SWA_KV_HEADS - 1:
                finish(n, halves)
                halves = []


def _swa(u, sink2):
    b, s, _ = u.shape
    blk = lambda name, w: U_COL[name] // w
    halo_blocks = SWA_TS // SWA_BLOCK
    return pl.pallas_call(
        _swa_kernel,
        grid=(b, s // SWA_TS),
        in_specs=[
            pl.BlockSpec((1, SWA_TS, SWA_WIDTH), lambda bi, i: (bi, i, blk("b_q", SWA_WIDTH))),
            pl.BlockSpec((1, SWA_TS, SWA_WIDTH), lambda bi, i: (bi, i, blk("b_z", SWA_WIDTH))),
            pl.BlockSpec((1, SWA_TS, 2 * SWA_KV_WIDTH),
                         lambda bi, i: (bi, i, blk("b_kv", 2 * SWA_KV_WIDTH))),
            pl.BlockSpec((1, SWA_BLOCK, 2 * SWA_KV_WIDTH),
                         lambda bi, i: (bi, jnp.maximum(i * halo_blocks - 1, 0),
                                        blk("b_kv", 2 * SWA_KV_WIDTH))),
            pl.BlockSpec((SWA_KV_HEADS, SWA_TILES * SWA_BLOCK), lambda bi, i: (0, 0)),
        ],
        out_specs=pl.BlockSpec((1, SWA_TS, SWA_WIDTH), lambda bi, i: (bi, i, 0)),
        out_shape=jax.ShapeDtypeStruct((b, s, SWA_WIDTH), BF16),
        compiler_params=pltpu.CompilerParams(
            dimension_semantics=("arbitrary", "arbitrary"),
            vmem_limit_bytes=V7X_VMEM_LIMIT_BYTES),
        name="swa",
    )(u, u, u, u, sink2)


MOBA_PAIRS = MOBA_WIDTH // LANES
MOBA_SEL_CHUNK = 1024
MOBA_TQ = 512
MOBA_G = MOBA_TQ // MOBA_BLOCK
POS_BIG = 1e30


def _head_split(q):
    lane = lax.broadcasted_iota(jnp.int32, q.shape, 1)
    return (jnp.where(lane < HEAD_DIM, q, 0), jnp.where(lane >= HEAD_DIM, q, 0))


def _moba_select_kernel(q_ref, k_ref, sel_ref, kmean_ref, *, nblk):
    s = k_ref.shape[1]

    def mean_body(j, carry):
        off = pl.multiple_of(j * MOBA_BLOCK, MOBA_BLOCK)
        kj = k_ref[0, pl.ds(off, MOBA_BLOCK), :].astype(F32)
        kmean_ref[pl.ds(j, 1), :] = jnp.sum(kj, axis=0, keepdims=True) * (1.0 / MOBA_BLOCK)
        return carry
    lax.fori_loop(0, nblk, mean_body, 0)

    km = kmean_ref[...]
    km_hi = km.astype(BF16)
    km_lo = (km - km_hi.astype(F32)).astype(BF16)
    ch = MOBA_SEL_CHUNK
    blk_id = lax.broadcasted_iota(jnp.int32, (nblk, ch), 0)
    blk_f = blk_id.astype(F32)
    col = lax.broadcasted_iota(jnp.int32, (nblk, ch), 1)

    def chunk_body(c, carry):
        off = pl.multiple_of(c * ch, ch)
        q_heads = _head_split(q_ref[0, pl.ds(off, ch), :])
        past = blk_id < lax.shift_right_logical(off + col, MOBA_BLOCK.bit_length() - 1)
        for h in range(2):
            g = (lax.dot_general(km_hi, q_heads[h], _NT, preferred_element_type=F32)
                 + lax.dot_general(km_lo, q_heads[h], _NT, preferred_element_type=F32))
            g = jnp.where(past, g, NEG_INF)
            sel = jnp.zeros((nblk, ch), jnp.bool_)
            for _ in range(MOBA_TOPK):
                mx = jnp.max(g, axis=0, keepdims=True)
                first = jnp.min(jnp.where(g == mx, blk_f, float(nblk)), axis=0, keepdims=True)
                pick = blk_f == first
                sel = sel | pick
                g = jnp.where(pick, -jnp.inf, g)
            sel_ref[0, h, :, pl.ds(off, ch)] = jnp.where(sel & past, 1.0, 0.0)
        return carry
    lax.fori_loop(0, s // ch, chunk_body, 0)


def _moba_select(u):
    b, s, _ = u.shape
    nblk = s // MOBA_BLOCK
    blk = lambda name: U_COL[name] // LANES
    return pl.pallas_call(
        functools.partial(_moba_select_kernel, nblk=nblk),
        grid=(b, MOBA_PAIRS),
        in_specs=[pl.BlockSpec((1, s, LANES), lambda bi, p: (bi, 0, blk("c_q") + p)),
                  pl.BlockSpec((1, s, LANES), lambda bi, p: (bi, 0, blk("c_k") + p))],
        out_specs=pl.BlockSpec((1, 2, nblk, s), lambda bi, p: (bi, p, 0, 0)),
        out_shape=jax.ShapeDtypeStruct((b, MOBA_HEADS, nblk, s), F32),
        scratch_shapes=[pltpu.VMEM((nblk, LANES), F32)],
        compiler_params=pltpu.CompilerParams(
            dimension_semantics=("arbitrary", "arbitrary"),
            vmem_limit_bytes=V7X_VMEM_LIMIT_BYTES),
        name="moba_select",
    )(u, u)


def _moba_kernel(q_ref, z_ref, sel_ref, k_ref, v_ref, o_ref, sa_ref, sb_ref):
    t = pl.program_id(2)
    q_heads = _head_split(q_ref[0])
    gk = MOBA_G * MOBA_BLOCK
    blocks = [slice(b * MOBA_BLOCK, (b + 1) * MOBA_BLOCK) for b in range(MOBA_G)]
    sum_rows = jnp.ones((SUM_ROWS, MOBA_BLOCK), BF16)

    def scores(g, h, rows, sc_ref):
        off = pl.multiple_of(g * gk, gk)
        st = lax.dot_general(k_ref[0, pl.ds(off, gk), :][rows], q_heads[h], _NT,
                             preferred_element_type=F32)
        sc_ref[h, rows, :] = st
        return jnp.max(st, axis=0, keepdims=True)

    def produce(g, sc_ref):
        return [[scores(g, h, rows, sc_ref) for rows in blocks] for h in range(2)]

    def step(g, sc_ref, maxima, on, mask, state, nxt):
        off = pl.multiple_of(g * gk, gk)
        vt = v_ref[0, pl.ds(off, gk), :].T
        new_state, next_maxima = [], []
        for h in range(2):
            m, l, acc = state[h]
            tiles = [sc_ref[h, rows, :] for rows in blocks]
            if mask is not None:
                tiles = [jnp.where(mask[rows], tile, NEG_INF) for tile, rows in zip(tiles, blocks)]
                block_max = [jnp.max(tile, axis=0, keepdims=True) for tile in tiles]
            else:
                block_max = maxima[h]
            ons = [o > 0.5 for o in on[h]]
            m_new = m
            for mb, o in zip(block_max, ons):
                m_new = jnp.maximum(m_new, jnp.where(o, mb, NEG_INF))
            pv = None
            head_maxima = []
            for tile, o, rows in zip(tiles, ons, blocks):
                p = jnp.exp2(tile - jnp.where(o, m_new, POS_BIG)).astype(BF16)
                vt_b = jnp.concatenate([vt[h * HEAD_DIM:(h + 1) * HEAD_DIM, rows], sum_rows], axis=0)
                part = jnp.dot(vt_b, p, preferred_element_type=F32)
                pv = part if pv is None else pv + part
                if nxt is not None:
                    head_maxima.append(scores(nxt[0], h, rows, nxt[1]))
            alpha = jnp.exp2(m - m_new)
            new_state.append((m_new, alpha * l + pv[HEAD_DIM:HEAD_DIM + 1],
                              alpha * acc + pv[:HEAD_DIM]))
            next_maxima.append(head_maxima)
        return tuple(new_state), next_maxima

    def half_step(g, cur_ref, nxt_ref, state, maxima):
        on = [[sel_ref[0, h, pl.ds(g * MOBA_G + b, 1), :] for b in range(MOBA_G)]
              for h in range(2)]
        return step(g, cur_ref, maxima, on, None, state, (g + 1, nxt_ref))

    def body(jj, carry):
        state, maxima = carry
        state, maxima = half_step(2 * jj, sa_ref, sb_ref, state, maxima)
        return half_step(2 * jj + 1, sb_ref, sa_ref, state, maxima)

    init = tuple((jnp.full((1, MOBA_TQ), NEG_INF, F32),
                  jnp.zeros((1, MOBA_TQ), F32),
                  jnp.zeros((HEAD_DIM, MOBA_TQ), F32)) for _ in range(2))
    state, maxima = lax.fori_loop(0, t // 2, body, (init, produce(0, sa_ref)))

    def own_group(sc_ref, state):
        kr = lax.broadcasted_iota(jnp.int32, (gk, MOBA_TQ), 0)
        qc = lax.broadcasted_iota(jnp.int32, (gk, MOBA_TQ), 1)
        second_half = lax.broadcasted_iota(jnp.int32, (1, MOBA_TQ), 1) >= MOBA_BLOCK
        ones = jnp.ones((1, MOBA_TQ), F32)
        on = [[jnp.where(second_half, sel_ref[0, h, pl.ds(t * MOBA_G, 1), :], 1.0), ones]
              for h in range(2)]
        return step(t, sc_ref, None, on, kr <= qc, state, None)[0]

    def odd_count(state, maxima):
        state, _ = half_step(t - 1, sa_ref, sb_ref, state, maxima)
        return own_group(sb_ref, state)

    def even_count(state, maxima):
        return own_group(sa_ref, state)

    state = lax.cond(lax.rem(t, 2) == 1, odd_count, even_count, state, maxima)


    ot = jnp.concatenate([acc / l for (_, l, acc) in state], axis=0)
    z = z_ref[0].astype(F32)
    o_ref[0] = (ot.T * _silu(z)).astype(BF16)


def _moba(u, sel):
    b, s, _ = u.shape
    nblk = s // MOBA_BLOCK
    blk = lambda name: U_COL[name] // LANES
    return pl.pallas_call(
        _moba_kernel,
        grid=(b, MOBA_PAIRS, s // MOBA_TQ),
        in_specs=[
            pl.BlockSpec((1, MOBA_TQ, LANES), lambda bi, p, t: (bi, t, blk("c_q") + p)),
            pl.BlockSpec((1, MOBA_TQ, LANES), lambda bi, p, t: (bi, t, blk("c_z") + p)),
            pl.BlockSpec((1, 2, nblk, MOBA_TQ), lambda bi, p, t: (bi, p, 0, t)),
            pl.BlockSpec((1, s, LANES), lambda bi, p, t: (bi, 0, blk("c_k") + p)),
            pl.BlockSpec((1, s, LANES), lambda bi, p, t: (bi, 0, blk("c_v") + p)),
        ],
        out_specs=pl.BlockSpec((1, MOBA_TQ, LANES), lambda bi, p, t: (bi, t, p)),
        out_shape=jax.ShapeDtypeStruct((b, s, MOBA_WIDTH), BF16),
        scratch_shapes=[pltpu.VMEM((2, MOBA_G * MOBA_BLOCK, MOBA_TQ), F32)] * 2,
        compiler_params=pltpu.CompilerParams(
            dimension_semantics=("arbitrary", "arbitrary", "arbitrary"),
            vmem_limit_bytes=V7X_VMEM_LIMIT_BYTES),
        name="moba",
    )(u, u, sel, u, u)


MERGE_TM = 512


def _merge_kernel(x_ref, ax_ref, axh_ref, az_ref, yb_ref, yc_ref, ga_ref, gb_ref, gc_ref,
                  pw_ref, ps_ref, wa_ref, wb_ref, wc_ref, wo_ref, fg_ref, o_ref, buf_ref,
                  *, final_norm):
    i = pl.program_id(1)
    tm = MERGE_TM
    buf_ref[0:POOL_HALO, :] = jnp.where(i > 0, axh_ref[0].astype(F32), 0.0)
    buf_ref[POOL_HALO:POOL_HALO + tm, :] = ax_ref[0].astype(F32)

    tpos = i * tm + lax.broadcasted_iota(jnp.int32, (tm, POOL_GROUP_DIM), 0)
    ya = []
    for gi, w in enumerate(POOL_WINDOWS):
        cols = slice(gi * POOL_GROUP_DIM, (gi + 1) * POOL_GROUP_DIM)
        cur = buf_ref[POOL_HALO:POOL_HALO + tm, cols]
        win = cur
        for sft in range(1, w):
            win = win + buf_ref[POOL_HALO - sft:POOL_HALO - sft + tm, cols]
        cnt = jnp.minimum(tpos + 1, w).astype(F32)
        pooled = (win / cnt - cur).astype(BF16)
        mixed = jnp.dot(pooled, pw_ref[gi], preferred_element_type=F32) * ps_ref[:, cols]
        ya.append((mixed * _silu(az_ref[0, :, cols].astype(F32))).astype(BF16))
    ya = jnp.concatenate(ya, axis=1)

    def gate(g_ref):
        return jnp.tanh(g_ref[0].astype(F32)) + 1.0

    merged = gate(ga_ref) * jnp.dot(ya, wa_ref[...], preferred_element_type=F32)
    merged += gate(gb_ref) * jnp.dot(yb_ref[0], wb_ref[...], preferred_element_type=F32)
    merged += gate(gc_ref) * jnp.dot(yc_ref[0], wc_ref[...], preferred_element_type=F32)
    out = x_ref[0] + jnp.dot(merged.astype(BF16), wo_ref[...], preferred_element_type=F32)
    if final_norm:
        ms = jnp.mean(out * out, axis=-1, keepdims=True)
        out = out * lax.rsqrt(ms + RMS_EPS) * fg_ref[...]
    o_ref[0] = out


def _merge(x, u, yb, yc, pool_w, pool_scale, wa, wb, wc, wo, fg, final_norm):
    b, s, d = x.shape
    tm = MERGE_TM
    ublk = lambda name, w: U_COL[name] // w
    gate_blk = U_COL["gates"] // d
    const = lambda shape: pl.BlockSpec(shape, lambda bi, i: (0,) * len(shape))
    row = lambda w, c: pl.BlockSpec((1, tm, w), lambda bi, i: (bi, i, c))
    return pl.pallas_call(
        functools.partial(_merge_kernel, final_norm=final_norm),
        grid=(b, s // tm),
        in_specs=[
            row(d, 0),
            row(POOL_WIDTH, ublk("a_x", POOL_WIDTH)),
            pl.BlockSpec((1, POOL_HALO, POOL_WIDTH),
                         lambda bi, i: (bi, jnp.maximum(i * (tm // POOL_HALO) - 1, 0),
                                        ublk("a_x", POOL_WIDTH))),
            row(POOL_WIDTH, ublk("a_z", POOL_WIDTH)),
            row(SWA_WIDTH, 0),
            row(MOBA_WIDTH, 0),
            row(d, gate_blk), row(d, gate_blk + 1), row(d, gate_blk + 2),
            const((POOL_GROUPS, POOL_GROUP_DIM, POOL_GROUP_DIM)),
            const((1, POOL_WIDTH)),
            const((POOL_WIDTH, d)), const((SWA_WIDTH, d)), const((MOBA_WIDTH, d)),
            const((d, d)),
            const((1, d)),
        ],
        out_specs=row(d, 0),
        out_shape=jax.ShapeDtypeStruct((b, s, d), F32),
        scratch_shapes=[pltpu.VMEM((POOL_HALO + tm, POOL_WIDTH), F32)],
        compiler_params=pltpu.CompilerParams(
            dimension_semantics=("arbitrary", "arbitrary"),
            vmem_limit_bytes=V7X_VMEM_LIMIT_BYTES),
        name="merge",
    )(x, u, u, u, yb, yc, u, u, u, pool_w, pool_scale, wa, wb, wc, wo, fg)


def kernel(x, norm_g, w_in, pool_w, pool_scale, sink_logits, w_proj_a, w_proj_b, w_proj_c,
           w_out, final_norm_g):
    depth = w_in.shape[0]
    assert x.shape[1] % max(INPROJ_TM, SWA_TS, MOBA_TQ, MOBA_SEL_CHUNK, MERGE_TM) == 0
    assert x.shape[2] == D_MODEL and w_in.shape[2] == sum(_REF_SIZES)
    fg = final_norm_g.reshape(1, D_MODEL)
    for l in range(depth):
        w_u = _u_weights(w_in[l])
        sink2 = (sink_logits[l].astype(F32) * LOG2E).reshape(SWA_KV_HEADS, SWA_TILES)
        sink2 = jnp.repeat(sink2, SWA_BLOCK, axis=1)
        wb = jnp.concatenate([w_proj_b[l][h * HEAD_DIM:(h + 1) * HEAD_DIM]
                              for h in _SWA_HEAD_ORDER], axis=0)
        u = _inproj(x, norm_g[l].reshape(1, D_MODEL), w_u)
        yb = _swa(u, sink2)
        yc = _moba(u, _moba_select(u))
        x = _merge(x, u, yb, yc,
                   pool_w[l].astype(BF16), pool_scale[l].reshape(1, POOL_WIDTH),
                   w_proj_a[l].astype(BF16), wb.astype(BF16),
                   w_proj_c[l].astype(BF16), (w_out[l] * GATE_SCALE).astype(BF16), fg,
                   final_norm=(l == depth - 1))
    return x
```

```python
import functools

import numpy as np
import jax
import jax.numpy as jnp
from jax import lax
from jax.experimental import pallas as pl
from jax.experimental.pallas import tpu as pltpu

F32 = jnp.float32
BF16 = jnp.bfloat16

LANES = 128
V7X_VMEM_LIMIT_BYTES = 56 * 1024 * 1024

D_MODEL = 1024
HEAD_DIM = 64
RMS_EPS = 1e-6
NEG_INF = -1e30
LOG2E = 1.4426950408889634
ATTN_SCALE = HEAD_DIM ** -0.5

POOL_WIDTH = 512
POOL_GROUPS = 4
POOL_GROUP_DIM = POOL_WIDTH // POOL_GROUPS
POOL_WINDOWS = (2, 4, 8, 16)
POOL_HALO = 16

SWA_Q_HEADS = 8
SWA_KV_HEADS = 2
SWA_BLOCK = 128
SWA_WIDTH = SWA_Q_HEADS * HEAD_DIM
SWA_KV_WIDTH = SWA_KV_HEADS * HEAD_DIM

MOBA_HEADS = 8
MOBA_BLOCK = 256
MOBA_TOPK = 3
MOBA_WIDTH = MOBA_HEADS * HEAD_DIM

N_BRANCH = 3

_REF_SIZES = (POOL_WIDTH, POOL_WIDTH, SWA_WIDTH, SWA_KV_WIDTH, SWA_KV_WIDTH, SWA_WIDTH,
              MOBA_WIDTH, MOBA_WIDTH, MOBA_WIDTH, MOBA_WIDTH, N_BRANCH * D_MODEL)
_REF_OFF = dict(zip(("a_x", "a_z", "b_q", "b_k", "b_v", "b_z", "c_q", "c_k", "c_v", "c_z", "gates"),
                    np.cumsum((0,) + _REF_SIZES[:-1]).tolist()))

_SWA_HEAD_ORDER = (0, 4, 1, 5, 2, 6, 3, 7)

U_COL = dict(a_x=0, a_z=512, b_q=1024, b_z=1536, c_q=2048, c_k=2560, c_v=3072, c_z=3584,
             gates=4096, b_kv=7168)
U_WIDTH = 7168 + 2 * SWA_KV_WIDTH


GATE_SCALE = 0.5


def _u_segments():
    qscale = ATTN_SCALE * LOG2E

    def swa_heads(name, scale):
        return [(_REF_OFF[name] + h * HEAD_DIM, HEAD_DIM, scale) for h in _SWA_HEAD_ORDER]

    segs = ([(_REF_OFF["a_x"], 2 * POOL_WIDTH, 1.0)]
            + swa_heads("b_q", qscale) + swa_heads("b_z", 1.0)
            + [(_REF_OFF["c_q"], MOBA_WIDTH, qscale),
               (_REF_OFF["c_k"], 3 * MOBA_WIDTH, 1.0),
               (_REF_OFF["gates"], N_BRANCH * D_MODEL, GATE_SCALE),
               (_REF_OFF["b_k"], 2 * SWA_KV_WIDTH, 1.0)])
    assert sum(w for _, w, _ in segs) == U_WIDTH
    return segs


_U_SEGMENTS = _u_segments()


def _u_weights(w):
    parts = [w[:, c0:c0 + n] if scale == 1.0 else w[:, c0:c0 + n] * np.float32(scale)
             for c0, n, scale in _U_SEGMENTS]
    return jnp.concatenate(parts, axis=1).astype(BF16)


def _sigmoid(z):
    return 0.5 * jnp.tanh(0.5 * z) + 0.5


def _silu(z):
    return z * _sigmoid(z)


_NT = (((1,), (1,)), ((), ()))
_TN = (((0,), (0,)), ((), ()))


INPROJ_TM = 512
INPROJ_CHUNK = 1024


def _inproj_kernel(x_ref, g_ref, w_ref, u_ref):
    x = x_ref[0]
    ms = jnp.mean(x * x, axis=-1, keepdims=True)
    h = (x * lax.rsqrt(ms + RMS_EPS) * g_ref[...]).astype(BF16)
    for c0 in range(0, U_WIDTH, INPROJ_CHUNK):
        c1 = min(c0 + INPROJ_CHUNK, U_WIDTH)
        u_ref[0, :, c0:c1] = jnp.dot(h, w_ref[:, c0:c1],
                                     preferred_element_type=F32).astype(BF16)


def _inproj(x, g, w):
    b, s, d = x.shape
    return pl.pallas_call(
        _inproj_kernel,
        grid=(b, s // INPROJ_TM),
        in_specs=[pl.BlockSpec((1, INPROJ_TM, d), lambda bi, i: (bi, i, 0)),
                  pl.BlockSpec((1, d), lambda bi, i: (0, 0)),
                  pl.BlockSpec((d, U_WIDTH), lambda bi, i: (0, 0))],
        out_specs=pl.BlockSpec((1, INPROJ_TM, U_WIDTH), lambda bi, i: (bi, i, 0)),
        out_shape=jax.ShapeDtypeStruct((b, s, U_WIDTH), BF16),
        compiler_params=pltpu.CompilerParams(
            dimension_semantics=("arbitrary", "arbitrary"),
            vmem_limit_bytes=V7X_VMEM_LIMIT_BYTES),
        name="inproj",
    )(x, g, w)


SWA_TS = 512
SWA_TILES = SWA_WIDTH // LANES
SWA_LOOKAHEAD = 2
SUM_ROWS = 16


def _swa_kernel(q_ref, z_ref, kv_ref, kvh_ref, sink_ref, o_ref):
    i = pl.program_id(1)
    nq = SWA_TILES * SWA_BLOCK
    lane = lax.broadcasted_iota(jnp.int32, (SWA_BLOCK, LANES), 1)
    half_mask = (lane < HEAD_DIM, lane >= HEAD_DIM)
    r = lax.broadcasted_iota(jnp.int32, (2 * SWA_BLOCK, nq), 0)
    c = lax.broadcasted_iota(jnp.int32, (2 * SWA_BLOCK, nq), 1) & (SWA_BLOCK - 1)
    band = (r > c) & (r <= c + SWA_BLOCK)
    sum_rows = jnp.ones((SUM_ROWS, 2 * SWA_BLOCK), BF16)

    def kv_pair(n):
        kv_prev = kvh_ref[0] if n == 0 else kv_ref[0, (n - 1) * SWA_BLOCK:n * SWA_BLOCK, :]
        return jnp.concatenate([kv_prev, kv_ref[0, n * SWA_BLOCK:(n + 1) * SWA_BLOCK, :]], axis=0)

    def qk(n, half):
        rows = slice(n * SWA_BLOCK, (n + 1) * SWA_BLOCK)
        qs = jnp.concatenate(
            [jnp.where(half_mask[half], q_ref[0, rows, t * LANES:(t + 1) * LANES], 0)
             for t in range(SWA_TILES)], axis=0)
        return lax.dot_general(kv_pair(n)[:, :SWA_KV_WIDTH], qs, _NT,
                               preferred_element_type=F32)

    def softmax_pv(n, half, st):
        valid = band & ((r >= SWA_BLOCK) | (i > 0)) if n == 0 else band
        st = jnp.where(valid, st, NEG_INF)
        sink = sink_ref[half:half + 1, :]
        m = jnp.maximum(jnp.max(st, axis=0, keepdims=True), sink)
        p = jnp.exp2(st - m).astype(BF16)
        vt = kv_pair(n)[:, SWA_KV_WIDTH:].T
        vt_h = jnp.concatenate([vt[half * HEAD_DIM:(half + 1) * HEAD_DIM], sum_rows], axis=0)
        pv = jnp.dot(vt_h, p, preferred_element_type=F32)
        return pv[:HEAD_DIM] / (pv[HEAD_DIM:HEAD_DIM + 1] + jnp.exp2(sink - m))

    def finish(n, halves):
        rows = slice(n * SWA_BLOCK, (n + 1) * SWA_BLOCK)
        ot = jnp.concatenate(halves, axis=0)
        for t in range(SWA_TILES):
            cols = slice(t * LANES, (t + 1) * LANES)
            o = ot[:, cols].T
            z = z_ref[0, rows, cols].astype(F32)
            o_ref[0, rows, cols] = (o * _silu(z)).astype(BF16)

    tiles = [(n, half) for n in range(SWA_TS // SWA_BLOCK) for half in range(SWA_KV_HEADS)]
    pending, halves = {}, []
    for idx in range(len(tiles) + SWA_LOOKAHEAD):
        if idx < len(tiles):
            pending[idx] = qk(*tiles[idx])
        done = idx - SWA_LOOKAHEAD
        if done >= 0:
            n, half = tiles[done]
            halves.append(softmax_pv(n, half, pending.pop(done)))
            if half == SWA_KV_HEADS - 1:
                finish(n, halves)
                halves = []


def _swa(u, sink2):
    b, s, _ = u.shape
    blk = lambda name, w: U_COL[name] // w
    halo_blocks = SWA_TS // SWA_BLOCK
    return pl.pallas_call(
        _swa_kernel,
        grid=(b, s // SWA_TS),
        in_specs=[
            pl.BlockSpec((1, SWA_TS, SWA_WIDTH), lambda bi, i: (bi, i, blk("b_q", SWA_WIDTH))),
            pl.BlockSpec((1, SWA_TS, SWA_WIDTH), lambda bi, i: (bi, i, blk("b_z", SWA_WIDTH))),
            pl.BlockSpec((1, SWA_TS, 2 * SWA_KV_WIDTH),
                         lambda bi, i: (bi, i, blk("b_kv", 2 * SWA_KV_WIDTH))),
            pl.BlockSpec((1, SWA_BLOCK, 2 * SWA_KV_WIDTH),
                         lambda bi, i: (bi, jnp.maximum(i * halo_blocks - 1, 0),
                                        blk("b_kv", 2 * SWA_KV_WIDTH))),
            pl.BlockSpec((SWA_KV_HEADS, SWA_TILES * SWA_BLOCK), lambda bi, i: (0, 0)),
        ],
        out_specs=pl.BlockSpec((1, SWA_TS, SWA_WIDTH), lambda bi, i: (bi, i, 0)),
        out_shape=jax.ShapeDtypeStruct((b, s, SWA_WIDTH), BF16),
        compiler_params=pltpu.CompilerParams(
            dimension_semantics=("arbitrary", "arbitrary"),
            vmem_limit_bytes=V7X_VMEM_LIMIT_BYTES),
        name="swa",
    )(u, u, u, u, sink2)


MOBA_PAIRS = MOBA_WIDTH // LANES
MOBA_SEL_CHUNK = 1024
MOBA_TQ = 512
MOBA_G = MOBA_TQ // MOBA_BLOCK
POS_BIG = 1e30
MOBA_UNROLL = 4
TILE_ORDER = tuple((h, b) for h in range(2) for b in range(MOBA_G))
PREFETCH_AFTER = ((TILE_ORDER[0], TILE_ORDER[1]), (TILE_ORDER[2],), (TILE_ORDER[3],), ())


def _head_split(q):
    lane = lax.broadcasted_iota(jnp.int32, q.shape, 1)
    return (jnp.where(lane < HEAD_DIM, q, 0), jnp.where(lane >= HEAD_DIM, q, 0))


def _moba_select_kernel(q_ref, k_ref, sel_ref, kmean_ref, *, nblk):
    s = k_ref.shape[1]

    def mean_body(j, carry):
        off = pl.multiple_of(j * MOBA_BLOCK, MOBA_BLOCK)
        kj = k_ref[0, pl.ds(off, MOBA_BLOCK), :].astype(F32)
        kmean_ref[pl.ds(j, 1), :] = jnp.sum(kj, axis=0, keepdims=True) * (1.0 / MOBA_BLOCK)
        return carry
    lax.fori_loop(0, nblk, mean_body, 0)

    km = kmean_ref[...]
    km_hi = km.astype(BF16)
    km_lo = (km - km_hi.astype(F32)).astype(BF16)
    ch = MOBA_SEL_CHUNK
    blk_id = lax.broadcasted_iota(jnp.int32, (nblk, ch), 0)
    blk_f = blk_id.astype(F32)
    col = lax.broadcasted_iota(jnp.int32, (nblk, ch), 1)

    def chunk_body(c, carry):
        off = pl.multiple_of(c * ch, ch)
        q_heads = _head_split(q_ref[0, pl.ds(off, ch), :])
        past = blk_id < lax.shift_right_logical(off + col, MOBA_BLOCK.bit_length() - 1)
        for h in range(2):
            g = (lax.dot_general(km_hi, q_heads[h], _NT, preferred_element_type=F32)
                 + lax.dot_general(km_lo, q_heads[h], _NT, preferred_element_type=F32))
            g = jnp.where(past, g, NEG_INF)
            sel = jnp.zeros((nblk, ch), jnp.bool_)
            for _ in range(MOBA_TOPK):
                mx = jnp.max(g, axis=0, keepdims=True)
                first = jnp.min(jnp.where(g == mx, blk_f, float(nblk)), axis=0, keepdims=True)
                pick = blk_f == first
                sel = sel | pick
                g = jnp.where(pick, -jnp.inf, g)
            sel_ref[0, h, :, pl.ds(off, ch)] = jnp.where(sel & past, 1.0, 0.0)
        return carry
    lax.fori_loop(0, s // ch, chunk_body, 0)


def _moba_select(u):
    b, s, _ = u.shape
    nblk = s // MOBA_BLOCK
    blk = lambda name: U_COL[name] // LANES
    return pl.pallas_call(
        functools.partial(_moba_select_kernel, nblk=nblk),
        grid=(b, MOBA_PAIRS),
        in_specs=[pl.BlockSpec((1, s, LANES), lambda bi, p: (bi, 0, blk("c_q") + p)),
                  pl.BlockSpec((1, s, LANES), lambda bi, p: (bi, 0, blk("c_k") + p))],
        out_specs=pl.BlockSpec((1, 2, nblk, s), lambda bi, p: (bi, p, 0, 0)),
        out_shape=jax.ShapeDtypeStruct((b, MOBA_HEADS, nblk, s), F32),
        scratch_shapes=[pltpu.VMEM((nblk, LANES), F32)],
        compiler_params=pltpu.CompilerParams(
            dimension_semantics=("arbitrary", "arbitrary"),
            vmem_limit_bytes=V7X_VMEM_LIMIT_BYTES),
        name="moba_select",
    )(u, u)


def _moba_kernel(q_ref, z_ref, sel_ref, k_ref, v_ref, o_ref, sa_ref, sb_ref):
    t = pl.program_id(2)
    q_heads = _head_split(q_ref[0])
    gk = MOBA_G * MOBA_BLOCK
    blocks = [slice(b * MOBA_BLOCK, (b + 1) * MOBA_BLOCK) for b in range(MOBA_G)]
    sum_rows = jnp.ones((SUM_ROWS, MOBA_BLOCK), BF16)

    def scores(g, h, rows, sc_ref):
        off = pl.multiple_of(g * gk, gk)
        st = lax.dot_general(k_ref[0, pl.ds(off, gk), :][rows], q_heads[h], _NT,
                             preferred_element_type=F32)
        sc_ref[h, rows, :] = st
        return jnp.max(st, axis=0, keepdims=True)

    def produce(g, sc_ref):
        return [[scores(g, h, rows, sc_ref) for rows in blocks] for h in range(2)]

    def step(g, sc_ref, maxima, on, mask, state, nxt):
        off = pl.multiple_of(g * gk, gk)
        vt = v_ref[0, pl.ds(off, gk), :].T
        masked, m_new = {}, []
        for h in range(2):
            if mask is not None:
                for b, rows in enumerate(blocks):
                    masked[h, b] = jnp.where(mask[rows], sc_ref[h, rows, :], NEG_INF)
                block_max = [jnp.max(masked[h, b], axis=0, keepdims=True) for b in range(MOBA_G)]
            else:
                block_max = maxima[h]
            mh = state[h][0]
            for mb, o in zip(block_max, on[h]):
                mh = jnp.maximum(mh, jnp.where(o > 0.5, mb, NEG_INF))
            m_new.append(mh)

        pv = [None, None]
        next_maxima = [[None] * MOBA_G for _ in range(2)]
        for idx, (h, b) in enumerate(TILE_ORDER):
            rows = blocks[b]
            tile = masked[h, b] if mask is not None else sc_ref[h, rows, :]
            p = jnp.exp2(tile - jnp.where(on[h][b] > 0.5, m_new[h], POS_BIG)).astype(BF16)
            vt_b = jnp.concatenate([vt[h * HEAD_DIM:(h + 1) * HEAD_DIM, rows], sum_rows], axis=0)
            part = jnp.dot(vt_b, p, preferred_element_type=F32)
            pv[h] = part if pv[h] is None else pv[h] + part
            if nxt is not None:
                for hn, bn in PREFETCH_AFTER[idx]:
                    next_maxima[hn][bn] = scores(nxt[0], hn, blocks[bn], nxt[1])

        new_state = []
        for h in range(2):
            m, l, acc = state[h]
            alpha = jnp.exp2(m - m_new[h])
            new_state.append((m_new[h], alpha * l + pv[h][HEAD_DIM:HEAD_DIM + 1],
                              alpha * acc + pv[h][:HEAD_DIM]))
        return tuple(new_state), next_maxima

    def half_step(g, cur_ref, nxt_ref, state, maxima):
        on = [[sel_ref[0, h, pl.ds(g * MOBA_G + b, 1), :] for b in range(MOBA_G)]
              for h in range(2)]
        return step(g, cur_ref, maxima, on, None, state, (g + 1, nxt_ref))

    buffers = (sa_ref, sb_ref)

    def run(first, count, state, maxima):
        for k in range(count):
            state, maxima = half_step(first + k, buffers[k % 2], buffers[(k + 1) % 2],
                                      state, maxima)
        return state, maxima

    def body(jj, carry):
        return run(MOBA_UNROLL * jj, MOBA_UNROLL, *carry)

    init = tuple((jnp.full((1, MOBA_TQ), NEG_INF, F32),
                  jnp.zeros((1, MOBA_TQ), F32),
                  jnp.zeros((HEAD_DIM, MOBA_TQ), F32)) for _ in range(2))
    state, maxima = lax.fori_loop(0, t // MOBA_UNROLL, body, (init, produce(0, sa_ref)))

    def own_group(sc_ref, state):
        kr = lax.broadcasted_iota(jnp.int32, (gk, MOBA_TQ), 0)
        qc = lax.broadcasted_iota(jnp.int32, (gk, MOBA_TQ), 1)
        second_half = lax.broadcasted_iota(jnp.int32, (1, MOBA_TQ), 1) >= MOBA_BLOCK
        ones = jnp.ones((1, MOBA_TQ), F32)
        on = [[jnp.where(second_half, sel_ref[0, h, pl.ds(t * MOBA_G, 1), :], 1.0), ones]
              for h in range(2)]
        return step(t, sc_ref, None, on, kr <= qc, state, None)[0]

    def finish(left):
        def branch(state, maxima):
            state, _ = run(t - left, left, state, maxima)
            return own_group(buffers[left % 2], state)
        return branch

    state = lax.switch(lax.rem(t, MOBA_UNROLL), [finish(r) for r in range(MOBA_UNROLL)],
                       state, maxima)

    ot = jnp.concatenate([acc / l for (_, l, acc) in state], axis=0)
    z = z_ref[0].astype(F32)
    o_ref[0] = (ot.T * _silu(z)).astype(BF16)


def _moba(u, sel):
    b, s, _ = u.shape
    nblk = s // MOBA_BLOCK
    blk = lambda name: U_COL[name] // LANES
    return pl.pallas_call(
        _moba_kernel,
        grid=(b, MOBA_PAIRS, s // MOBA_TQ),
        in_specs=[
            pl.BlockSpec((1, MOBA_TQ, LANES), lambda bi, p, t: (bi, t, blk("c_q") + p)),
            pl.BlockSpec((1, MOBA_TQ, LANES), lambda bi, p, t: (bi, t, blk("c_z") + p)),
            pl.BlockSpec((1, 2, nblk, MOBA_TQ), lambda bi, p, t: (bi, p, 0, t)),
            pl.BlockSpec((1, s, LANES), lambda bi, p, t: (bi, 0, blk("c_k") + p)),
            pl.BlockSpec((1, s, LANES), lambda bi, p, t: (bi, 0, blk("c_v") + p)),
        ],
        out_specs=pl.BlockSpec((1, MOBA_TQ, LANES), lambda bi, p, t: (bi, t, p)),
        out_shape=jax.ShapeDtypeStruct((b, s, MOBA_WIDTH), BF16),
        scratch_shapes=[pltpu.VMEM((2, MOBA_G * MOBA_BLOCK, MOBA_TQ), F32)] * 2,
        compiler_params=pltpu.CompilerParams(
            dimension_semantics=("arbitrary", "arbitrary", "arbitrary"),
            vmem_limit_bytes=V7X_VMEM_LIMIT_BYTES),
        name="moba",
    )(u, u, sel, u, u)


MERGE_TM = 512


def _merge_kernel(x_ref, ax_ref, axh_ref, az_ref, yb_ref, yc_ref, ga_ref, gb_ref, gc_ref,
                  pw_ref, ps_ref, wa_ref, wb_ref, wc_ref, wo_ref, fg_ref, o_ref, buf_ref,
                  *, final_norm):
    i = pl.program_id(1)
    tm = MERGE_TM
    buf_ref[0:POOL_HALO, :] = jnp.where(i > 0, axh_ref[0].astype(F32), 0.0)
    buf_ref[POOL_HALO:POOL_HALO + tm, :] = ax_ref[0].astype(F32)

    tpos = i * tm + lax.broadcasted_iota(jnp.int32, (tm, POOL_GROUP_DIM), 0)
    ya = []
    for gi, w in enumerate(POOL_WINDOWS):
        cols = slice(gi * POOL_GROUP_DIM, (gi + 1) * POOL_GROUP_DIM)
        cur = buf_ref[POOL_HALO:POOL_HALO + tm, cols]
        win = cur
        for sft in range(1, w):
            win = win + buf_ref[POOL_HALO - sft:POOL_HALO - sft + tm, cols]
        cnt = jnp.minimum(tpos + 1, w).astype(F32)
        pooled = (win / cnt - cur).astype(BF16)
        mixed = jnp.dot(pooled, pw_ref[gi], preferred_element_type=F32) * ps_ref[:, cols]
        ya.append((mixed * _silu(az_ref[0, :, cols].astype(F32))).astype(BF16))
    ya = jnp.concatenate(ya, axis=1)

    def gate(g_ref):
        return jnp.tanh(g_ref[0].astype(F32)) + 1.0

    merged = gate(ga_ref) * jnp.dot(ya, wa_ref[...], preferred_element_type=F32)
    merged += gate(gb_ref) * jnp.dot(yb_ref[0], wb_ref[...], preferred_element_type=F32)
    merged += gate(gc_ref) * jnp.dot(yc_ref[0], wc_ref[...], preferred_element_type=F32)
    out = x_ref[0] + jnp.dot(merged.astype(BF16), wo_ref[...], preferred_element_type=F32)
    if final_norm:
        ms = jnp.mean(out * out, axis=-1, keepdims=True)
        out = out * lax.rsqrt(ms + RMS_EPS) * fg_ref[...]
    o_ref[0] = out


def _merge(x, u, yb, yc, pool_w, pool_scale, wa, wb, wc, wo, fg, final_norm):
    b, s, d = x.shape
    tm = MERGE_TM
    ublk = lambda name, w: U_COL[name] // w
    gate_blk = U_COL["gates"] // d
    const = lambda shape: pl.BlockSpec(shape, lambda bi, i: (0,) * len(shape))
    row = lambda w, c: pl.BlockSpec((1, tm, w), lambda bi, i: (bi, i, c))
    return pl.pallas_call(
        functools.partial(_merge_kernel, final_norm=final_norm),
        grid=(b, s // tm),
        in_specs=[
            row(d, 0),
            row(POOL_WIDTH, ublk("a_x", POOL_WIDTH)),
            pl.BlockSpec((1, POOL_HALO, POOL_WIDTH),
                         lambda bi, i: (bi, jnp.maximum(i * (tm // POOL_HALO) - 1, 0),
                                        ublk("a_x", POOL_WIDTH))),
            row(POOL_WIDTH, ublk("a_z", POOL_WIDTH)),
            row(SWA_WIDTH, 0),
            row(MOBA_WIDTH, 0),
            row(d, gate_blk), row(d, gate_blk + 1), row(d, gate_blk + 2),
            const((POOL_GROUPS, POOL_GROUP_DIM, POOL_GROUP_DIM)),
            const((1, POOL_WIDTH)),
            const((POOL_WIDTH, d)), const((SWA_WIDTH, d)), const((MOBA_WIDTH, d)),
            const((d, d)),
            const((1, d)),
        ],
        out_specs=row(d, 0),
        out_shape=jax.ShapeDtypeStruct((b, s, d), F32),
        scratch_shapes=[pltpu.VMEM((POOL_HALO + tm, POOL_WIDTH), F32)],
        compiler_params=pltpu.CompilerParams(
            dimension_semantics=("arbitrary", "arbitrary"),
            vmem_limit_bytes=V7X_VMEM_LIMIT_BYTES),
        name="merge",
    )(x, u, u, u, yb, yc, u, u, u, pool_w, pool_scale, wa, wb, wc, wo, fg)


def kernel(x, norm_g, w_in, pool_w, pool_scale, sink_logits, w_proj_a, w_proj_b, w_proj_c,
           w_out, final_norm_g):
    depth = w_in.shape[0]
    assert x.shape[1] % max(INPROJ_TM, SWA_TS, MOBA_TQ, MOBA_SEL_CHUNK, MERGE_TM) == 0
    assert x.shape[2] == D_MODEL and w_in.shape[2] == sum(_REF_SIZES)
    fg = final_norm_g.reshape(1, D_MODEL)
    for l in range(depth):
        w_u = _u_weights(w_in[l])
        sink2 = (sink_logits[l].astype(F32) * LOG2E).reshape(SWA_KV_HEADS, SWA_TILES)
        sink2 = jnp.repeat(sink2, SWA_BLOCK, axis=1)
        wb = jnp.concatenate([w_proj_b[l][h * HEAD_DIM:(h + 1) * HEAD_DIM]
                              for h in _SWA_HEAD_ORDER], axis=0)
        u = _inproj(x, norm_g[l].reshape(1, D_MODEL), w_u)
        yb = _swa(u, sink2)
        yc = _moba(u, _moba_select(u))
        x = _merge(x, u, yb, yc,
                   pool_w[l].astype(BF16), pool_scale[l].reshape(1, POOL_WIDTH),
                   w_proj_a[l].astype(BF16), wb.astype(BF16),
                   w_proj_c[l].astype(BF16), (w_out[l] * GATE_SCALE).astype(BF16), fg,
                   final_norm=(l == depth - 1))
    return x
```

```python
import functools

import numpy as np
import jax
import jax.numpy as jnp
from jax import lax
from jax.experimental import pallas as pl
from jax.experimental.pallas import tpu as pltpu

F32 = jnp.float32
BF16 = jnp.bfloat16

LANES = 128
V7X_VMEM_LIMIT_BYTES = 56 * 1024 * 1024

D_MODEL = 1024
HEAD_DIM = 64
RMS_EPS = 1e-6
NEG_INF = -1e30
LOG2E = 1.4426950408889634
ATTN_SCALE = HEAD_DIM ** -0.5

POOL_WIDTH = 512
POOL_GROUPS = 4
POOL_GROUP_DIM = POOL_WIDTH // POOL_GROUPS
POOL_WINDOWS = (2, 4, 8, 16)
POOL_HALO = 16

SWA_Q_HEADS = 8
SWA_KV_HEADS = 2
SWA_BLOCK = 128
SWA_WIDTH = SWA_Q_HEADS * HEAD_DIM
SWA_KV_WIDTH = SWA_KV_HEADS * HEAD_DIM

MOBA_HEADS = 8
MOBA_BLOCK = 256
MOBA_TOPK = 3
MOBA_WIDTH = MOBA_HEADS * HEAD_DIM

N_BRANCH = 3

_REF_SIZES = (POOL_WIDTH, POOL_WIDTH, SWA_WIDTH, SWA_KV_WIDTH, SWA_KV_WIDTH, SWA_WIDTH,
              MOBA_WIDTH, MOBA_WIDTH, MOBA_WIDTH, MOBA_WIDTH, N_BRANCH * D_MODEL)
_REF_OFF = dict(zip(("a_x", "a_z", "b_q", "b_k", "b_v", "b_z", "c_q", "c_k", "c_v", "c_z", "gates"),
                    np.cumsum((0,) + _REF_SIZES[:-1]).tolist()))

U_COL = dict(a_x=0, a_z=512, b_q=1024, b_z=1536, c_q=2048, c_k=2560, c_v=3072, c_z=3584,
             gates=4096, b_kv=7168)
U_WIDTH = 7168 + 2 * SWA_KV_WIDTH

GATE_SCALE = 0.5


def _swa_head_major(a, axis):
    shape = a.shape
    a = a.reshape(shape[:axis] + (SWA_KV_HEADS, SWA_Q_HEADS // SWA_KV_HEADS, HEAD_DIM)
                  + shape[axis + 1:])
    return jnp.swapaxes(a, axis, axis + 1).reshape(shape)


def _u_weights(w):
    qscale = np.float32(ATTN_SCALE * LOG2E)

    def ref(name, n):
        return w[:, _REF_OFF[name]:_REF_OFF[name] + n]

    parts = [ref("a_x", 2 * POOL_WIDTH),
             _swa_head_major(ref("b_q", SWA_WIDTH), 1) * qscale,
             _swa_head_major(ref("b_z", SWA_WIDTH), 1),
             ref("c_q", MOBA_WIDTH) * qscale,
             ref("c_k", 3 * MOBA_WIDTH),
             ref("gates", N_BRANCH * D_MODEL) * np.float32(GATE_SCALE),
             ref("b_k", 2 * SWA_KV_WIDTH)]
    out = jnp.concatenate(parts, axis=1).astype(BF16)
    assert out.shape[1] == U_WIDTH
    return out


def _sigmoid(z):
    return 0.5 * jnp.tanh(0.5 * z) + 0.5


def _silu(z):
    return z * _sigmoid(z)


_NT = (((1,), (1,)), ((), ()))


INPROJ_TM = 512
INPROJ_CHUNK = 1024


def _inproj_kernel(x_ref, g_ref, w_ref, u_ref):
    x = x_ref[0]
    ms = jnp.mean(x * x, axis=-1, keepdims=True)
    h = (x * lax.rsqrt(ms + RMS_EPS) * g_ref[...]).astype(BF16)
    for c0 in range(0, U_WIDTH, INPROJ_CHUNK):
        c1 = min(c0 + INPROJ_CHUNK, U_WIDTH)
        u_ref[0, :, c0:c1] = jnp.dot(h, w_ref[:, c0:c1],
                                     preferred_element_type=F32).astype(BF16)


def _inproj(x, g, w):
    b, s, d = x.shape
    return pl.pallas_call(
        _inproj_kernel,
        grid=(b, s // INPROJ_TM),
        in_specs=[pl.BlockSpec((1, INPROJ_TM, d), lambda bi, i: (bi, i, 0)),
                  pl.BlockSpec((1, d), lambda bi, i: (0, 0)),
                  pl.BlockSpec((d, U_WIDTH), lambda bi, i: (0, 0))],
        out_specs=pl.BlockSpec((1, INPROJ_TM, U_WIDTH), lambda bi, i: (bi, i, 0)),
        out_shape=jax.ShapeDtypeStruct((b, s, U_WIDTH), BF16),
        compiler_params=pltpu.CompilerParams(
            dimension_semantics=("arbitrary", "arbitrary"),
            vmem_limit_bytes=V7X_VMEM_LIMIT_BYTES),
        name="inproj",
    )(x, g, w)


SWA_TS = 512
SWA_TILES = SWA_WIDTH // LANES
SWA_LOOKAHEAD = 2
SUM_ROWS = 16


def _swa_kernel(q_ref, z_ref, kv_ref, kvh_ref, sink_ref, o_ref):
    i = pl.program_id(1)
    nq = SWA_TILES * SWA_BLOCK
    lane = lax.broadcasted_iota(jnp.int32, (SWA_BLOCK, LANES), 1)
    half_mask = (lane < HEAD_DIM, lane >= HEAD_DIM)
    r = lax.broadcasted_iota(jnp.int32, (2 * SWA_BLOCK, nq), 0)
    c = lax.broadcasted_iota(jnp.int32, (2 * SWA_BLOCK, nq), 1) & (SWA_BLOCK - 1)
    band = (r > c) & (r <= c + SWA_BLOCK)
    sum_rows = jnp.ones((SUM_ROWS, 2 * SWA_BLOCK), BF16)

    def kv_pair(n):
        kv_prev = kvh_ref[0] if n == 0 else kv_ref[0, (n - 1) * SWA_BLOCK:n * SWA_BLOCK, :]
        return jnp.concatenate([kv_prev, kv_ref[0, n * SWA_BLOCK:(n + 1) * SWA_BLOCK, :]], axis=0)

    def qk(n, half):
        rows = slice(n * SWA_BLOCK, (n + 1) * SWA_BLOCK)
        qs = jnp.concatenate(
            [jnp.where(half_mask[half], q_ref[0, rows, t * LANES:(t + 1) * LANES], 0)
             for t in range(SWA_TILES)], axis=0)
        return lax.dot_general(kv_pair(n)[:, :SWA_KV_WIDTH], qs, _NT,
                               preferred_element_type=F32)

    def softmax_pv(n, half, st):
        valid = band & ((r >= SWA_BLOCK) | (i > 0)) if n == 0 else band
        st = jnp.where(valid, st, NEG_INF)
        sink = sink_ref[half:half + 1, :]
        m = jnp.maximum(jnp.max(st, axis=0, keepdims=True), sink)
        p = jnp.exp2(st - m).astype(BF16)
        vt = kv_pair(n)[:, SWA_KV_WIDTH:].T
        vt_h = jnp.concatenate([vt[half * HEAD_DIM:(half + 1) * HEAD_DIM], sum_rows], axis=0)
        pv = jnp.dot(vt_h, p, preferred_element_type=F32)
        return pv[:HEAD_DIM] / (pv[HEAD_DIM:HEAD_DIM + 1] + jnp.exp2(sink - m))

    def finish(n, halves):
        rows = slice(n * SWA_BLOCK, (n + 1) * SWA_BLOCK)
        ot = jnp.concatenate(halves, axis=0)
        for t in range(SWA_TILES):
            cols = slice(t * LANES, (t + 1) * LANES)
            o = ot[:, cols].T
            z = z_ref[0, rows, cols].astype(F32)
            o_ref[0, rows, cols] = (o * _silu(z)).astype(BF16)

    tiles = [(n, half) for n in range(SWA_TS // SWA_BLOCK) for half in range(SWA_KV_HEADS)]
    pending, halves = {}, []
    for idx in range(len(tiles) + SWA_LOOKAHEAD):
        if idx < len(tiles):
            pending[idx] = qk(*tiles[idx])
        done = idx - SWA_LOOKAHEAD
        if done >= 0:
            n, half = tiles[done]
            halves.append(softmax_pv(n, half, pending.pop(done)))
            if half == SWA_KV_HEADS - 1:
                finish(n, halves)
                halves = []


def _swa(u, sink2):
    b, s, _ = u.shape
    blk = lambda name, w: U_COL[name] // w
    halo_blocks = SWA_TS // SWA_BLOCK
    return pl.pallas_call(
        _swa_kernel,
        grid=(b, s // SWA_TS),
        in_specs=[
            pl.BlockSpec((1, SWA_TS, SWA_WIDTH), lambda bi, i: (bi, i, blk("b_q", SWA_WIDTH))),
            pl.BlockSpec((1, SWA_TS, SWA_WIDTH), lambda bi, i: (bi, i, blk("b_z", SWA_WIDTH))),
            pl.BlockSpec((1, SWA_TS, 2 * SWA_KV_WIDTH),
                         lambda bi, i: (bi, i, blk("b_kv", 2 * SWA_KV_WIDTH))),
            pl.BlockSpec((1, SWA_BLOCK, 2 * SWA_KV_WIDTH),
                         lambda bi, i: (bi, jnp.maximum(i * halo_blocks - 1, 0),
                                        blk("b_kv", 2 * SWA_KV_WIDTH))),
            pl.BlockSpec((SWA_KV_HEADS, SWA_TILES * SWA_BLOCK), lambda bi, i: (0, 0)),
        ],
        out_specs=pl.BlockSpec((1, SWA_TS, SWA_WIDTH), lambda bi, i: (bi, i, 0)),
        out_shape=jax.ShapeDtypeStruct((b, s, SWA_WIDTH), BF16),
        compiler_params=pltpu.CompilerParams(
            dimension_semantics=("arbitrary", "arbitrary"),
            vmem_limit_bytes=V7X_VMEM_LIMIT_BYTES),
        name="swa",
    )(u, u, u, u, sink2)


MOBA_PAIRS = MOBA_WIDTH // LANES
MOBA_SEL_CHUNK = 1024
MOBA_TQ = 512
MOBA_G = MOBA_TQ // MOBA_BLOCK
MOBA_TILES = 2
MOBA_UNROLL = 4
POS_BIG = 1e30
TILE_ORDER = tuple((h, b) for h in range(2) for b in range(MOBA_G))
PREFETCH_AFTER = ((TILE_ORDER[0], TILE_ORDER[1]), (TILE_ORDER[2],), (TILE_ORDER[3],), ())


def _head_split(q):
    lane = lax.broadcasted_iota(jnp.int32, q.shape, 1)
    return (jnp.where(lane < HEAD_DIM, q, 0), jnp.where(lane >= HEAD_DIM, q, 0))


def _moba_select_kernel(q_ref, k_ref, sel_ref, kmean_ref, *, nblk):
    s = k_ref.shape[1]

    def mean_body(j, carry):
        off = pl.multiple_of(j * MOBA_BLOCK, MOBA_BLOCK)
        kj = k_ref[0, pl.ds(off, MOBA_BLOCK), :].astype(F32)
        kmean_ref[pl.ds(j, 1), :] = jnp.sum(kj, axis=0, keepdims=True) * (1.0 / MOBA_BLOCK)
        return carry
    lax.fori_loop(0, nblk, mean_body, 0)

    km = kmean_ref[...]
    km_hi = km.astype(BF16)
    km_lo = (km - km_hi.astype(F32)).astype(BF16)
    ch = MOBA_SEL_CHUNK
    blk_id = lax.broadcasted_iota(jnp.int32, (nblk, ch), 0)
    blk_f = blk_id.astype(F32)
    col = lax.broadcasted_iota(jnp.int32, (nblk, ch), 1)

    def chunk_body(c, carry):
        off = pl.multiple_of(c * ch, ch)
        q_heads = _head_split(q_ref[0, pl.ds(off, ch), :])
        past = blk_id < lax.shift_right_logical(off + col, MOBA_BLOCK.bit_length() - 1)
        for h in range(2):
            g = (lax.dot_general(km_hi, q_heads[h], _NT, preferred_element_type=F32)
                 + lax.dot_general(km_lo, q_heads[h], _NT, preferred_element_type=F32))
            g = jnp.where(past, g, NEG_INF)
            sel = jnp.zeros((nblk, ch), jnp.bool_)
            for _ in range(MOBA_TOPK):
                mx = jnp.max(g, axis=0, keepdims=True)
                first = jnp.min(jnp.where(g == mx, blk_f, float(nblk)), axis=0, keepdims=True)
                pick = blk_f == first
                sel = sel | pick
                g = jnp.where(pick, -jnp.inf, g)
            sel_ref[0, h, :, pl.ds(off, ch)] = jnp.where(sel & past, 1.0, 0.0)
        return carry
    lax.fori_loop(0, s // ch, chunk_body, 0)


def _moba_select(u):
    b, s, _ = u.shape
    nblk = s // MOBA_BLOCK
    blk = lambda name: U_COL[name] // LANES
    return pl.pallas_call(
        functools.partial(_moba_select_kernel, nblk=nblk),
        grid=(b, MOBA_PAIRS),
        in_specs=[pl.BlockSpec((1, s, LANES), lambda bi, p: (bi, 0, blk("c_q") + p)),
                  pl.BlockSpec((1, s, LANES), lambda bi, p: (bi, 0, blk("c_k") + p))],
        out_specs=pl.BlockSpec((1, 2, nblk, s), lambda bi, p: (bi, p, 0, 0)),
        out_shape=jax.ShapeDtypeStruct((b, MOBA_HEADS, nblk, s), F32),
        scratch_shapes=[pltpu.VMEM((nblk, LANES), F32)],
        compiler_params=pltpu.CompilerParams(
            dimension_semantics=("arbitrary", "arbitrary"),
            vmem_limit_bytes=V7X_VMEM_LIMIT_BYTES),
        name="moba_select",
    )(u, u)


def _moba_kernel(q_ref, z_ref, sel_ref, k_ref, v_ref, o_ref, sa_ref, sb_ref):
    gk = MOBA_G * MOBA_BLOCK
    blocks = [slice(b * MOBA_BLOCK, (b + 1) * MOBA_BLOCK) for b in range(MOBA_G)]
    sum_rows = jnp.ones((SUM_ROWS, MOBA_BLOCK), BF16)

    def scores(qh, g, h, rows, sc_ref):
        off = pl.multiple_of(g * gk, gk)
        st = lax.dot_general(k_ref[0, pl.ds(off, gk), :][rows], qh[h], _NT,
                             preferred_element_type=F32)
        sc_ref[h, rows, :] = st
        return jnp.max(st, axis=0, keepdims=True)

    def step(g, sc_ref, maxima, on, mask, state, nxt):
        off = pl.multiple_of(g * gk, gk)
        vt = v_ref[0, pl.ds(off, gk), :].T
        masked, m_new = {}, []
        for h in range(2):
            if mask is not None:
                for b, rows in enumerate(blocks):
                    masked[h, b] = jnp.where(mask[rows], sc_ref[h, rows, :], NEG_INF)
                block_max = [jnp.max(masked[h, b], axis=0, keepdims=True) for b in range(MOBA_G)]
            else:
                block_max = maxima[h]
            mh = state[h][0]
            for mb, o in zip(block_max, on[h]):
                mh = jnp.maximum(mh, jnp.where(o > 0.5, mb, NEG_INF))
            m_new.append(mh)

        pv = [None, None]
        next_maxima = [[None] * MOBA_G for _ in range(2)]
        for idx, (h, b) in enumerate(TILE_ORDER):
            rows = blocks[b]
            tile = masked[h, b] if mask is not None else sc_ref[h, rows, :]
            p = jnp.exp2(tile - jnp.where(on[h][b] > 0.5, m_new[h], POS_BIG)).astype(BF16)
            vt_b = jnp.concatenate([vt[h * HEAD_DIM:(h + 1) * HEAD_DIM, rows], sum_rows], axis=0)
            part = jnp.dot(vt_b, p, preferred_element_type=F32)
            pv[h] = part if pv[h] is None else pv[h] + part
            if nxt is not None:
                for hn, bn in PREFETCH_AFTER[idx]:
                    next_maxima[hn][bn] = scores(nxt[0], nxt[1], hn, blocks[bn], nxt[2])

        new_state = []
        for h in range(2):
            m, l, acc = state[h]
            alpha = jnp.exp2(m - m_new[h])
            new_state.append((m_new[h], alpha * l + pv[h][HEAD_DIM:HEAD_DIM + 1],
                              alpha * acc + pv[h][:HEAD_DIM]))
        return tuple(new_state), next_maxima

    def attend(tile, t, qh, bufs, maxima, nxt_qh):
        cols = slice(tile * MOBA_TQ, (tile + 1) * MOBA_TQ)

        def half_step(g, cur_ref, nxt_ref, state, maxima):
            on = [[sel_ref[0, h, pl.ds(g * MOBA_G + b, 1), cols] for b in range(MOBA_G)]
                  for h in range(2)]
            return step(g, cur_ref, maxima, on, None, state, (qh, g + 1, nxt_ref))

        def run(first, count, state, maxima):
            for k in range(count):
                state, maxima = half_step(first + k, bufs[k % 2], bufs[(k + 1) % 2], state, maxima)
            return state, maxima

        init = tuple((jnp.full((1, MOBA_TQ), NEG_INF, F32),
                      jnp.zeros((1, MOBA_TQ), F32),
                      jnp.zeros((HEAD_DIM, MOBA_TQ), F32)) for _ in range(2))
        state, maxima = lax.fori_loop(
            0, t // MOBA_UNROLL, lambda jj, c: run(MOBA_UNROLL * jj, MOBA_UNROLL, *c), (init, maxima))

        def own_group(sc_ref, free_ref, state):
            kr = lax.broadcasted_iota(jnp.int32, (gk, MOBA_TQ), 0)
            qc = lax.broadcasted_iota(jnp.int32, (gk, MOBA_TQ), 1)
            second_half = lax.broadcasted_iota(jnp.int32, (1, MOBA_TQ), 1) >= MOBA_BLOCK
            ones = jnp.ones((1, MOBA_TQ), F32)
            on = [[jnp.where(second_half, sel_ref[0, h, pl.ds(t * MOBA_G, 1), cols], 1.0), ones]
                  for h in range(2)]
            nxt = None if nxt_qh is None else (nxt_qh, 0, free_ref)
            return step(t, sc_ref, None, on, kr <= qc, state, nxt)

        def finish(left):
            def branch(state, maxima):
                state, _ = run(t - left, left, state, maxima)
                state, next_maxima = own_group(bufs[left % 2], bufs[(left + 1) % 2], state)
                return (state, next_maxima) if nxt_qh is not None else state
            return branch

        lefts = [r for r in range(MOBA_UNROLL) if r % MOBA_TILES == tile]
        out = lax.switch(lax.rem(t, MOBA_UNROLL) // MOBA_TILES, [finish(r) for r in lefts],
                         state, maxima)
        state, next_maxima = out if nxt_qh is not None else (out, None)

        ot = jnp.concatenate([acc / l for (_, l, acc) in state], axis=0)
        z = z_ref[0, cols, :].astype(F32)
        o_ref[0, cols, :] = (ot.T * _silu(z)).astype(BF16)
        return next_maxima

    t0 = MOBA_TILES * pl.program_id(2)
    qh0 = _head_split(q_ref[0, 0:MOBA_TQ, :])
    qh1 = _head_split(q_ref[0, MOBA_TQ:2 * MOBA_TQ, :])
    maxima = [[scores(qh0, 0, h, rows, sa_ref) for rows in blocks] for h in range(2)]
    maxima = attend(0, t0, qh0, (sa_ref, sb_ref), maxima, qh1)
    attend(1, t0 + 1, qh1, (sb_ref, sa_ref), maxima, None)


def _moba(u, sel):
    b, s, _ = u.shape
    nblk = s // MOBA_BLOCK
    blk = lambda name: U_COL[name] // LANES
    tq = MOBA_TILES * MOBA_TQ
    return pl.pallas_call(
        _moba_kernel,
        grid=(b, MOBA_PAIRS, s // tq),
        in_specs=[
            pl.BlockSpec((1, tq, LANES), lambda bi, p, t: (bi, t, blk("c_q") + p)),
            pl.BlockSpec((1, tq, LANES), lambda bi, p, t: (bi, t, blk("c_z") + p)),
            pl.BlockSpec((1, 2, nblk, tq), lambda bi, p, t: (bi, p, 0, t)),
            pl.BlockSpec((1, s, LANES), lambda bi, p, t: (bi, 0, blk("c_k") + p)),
            pl.BlockSpec((1, s, LANES), lambda bi, p, t: (bi, 0, blk("c_v") + p)),
        ],
        out_specs=pl.BlockSpec((1, tq, LANES), lambda bi, p, t: (bi, t, p)),
        out_shape=jax.ShapeDtypeStruct((b, s, MOBA_WIDTH), BF16),
        scratch_shapes=[pltpu.VMEM((2, MOBA_G * MOBA_BLOCK, MOBA_TQ), F32)] * 2,
        compiler_params=pltpu.CompilerParams(
            dimension_semantics=("arbitrary", "arbitrary", "arbitrary"),
            vmem_limit_bytes=V7X_VMEM_LIMIT_BYTES),
        name="moba",
    )(u, u, sel, u, u)


MERGE_TM = 512


def _merge_kernel(x_ref, ax_ref, axh_ref, az_ref, yb_ref, yc_ref, ga_ref, gb_ref, gc_ref,
                  pw_ref, ps_ref, wa_ref, wb_ref, wc_ref, wo_ref, fg_ref, o_ref, buf_ref,
                  *, final_norm):
    i = pl.program_id(1)
    tm = MERGE_TM
    buf_ref[0:POOL_HALO, :] = jnp.where(i > 0, axh_ref[0].astype(F32), 0.0)
    buf_ref[POOL_HALO:POOL_HALO + tm, :] = ax_ref[0].astype(F32)

    tpos = i * tm + lax.broadcasted_iota(jnp.int32, (tm, POOL_GROUP_DIM), 0)
    ya = []
    for gi, w in enumerate(POOL_WINDOWS):
        cols = slice(gi * POOL_GROUP_DIM, (gi + 1) * POOL_GROUP_DIM)
        cur = buf_ref[POOL_HALO:POOL_HALO + tm, cols]
        win = cur
        for sft in range(1, w):
            win = win + buf_ref[POOL_HALO - sft:POOL_HALO - sft + tm, cols]
        cnt = jnp.minimum(tpos + 1, w).astype(F32)
        pooled = (win / cnt - cur).astype(BF16)
        mixed = jnp.dot(pooled, pw_ref[gi], preferred_element_type=F32) * ps_ref[:, cols]
        ya.append((mixed * _silu(az_ref[0, :, cols].astype(F32))).astype(BF16))
    ya = jnp.concatenate(ya, axis=1)

    def gate(g_ref):
        return jnp.tanh(g_ref[0].astype(F32)) + 1.0

    merged = gate(ga_ref) * jnp.dot(ya, wa_ref[...], preferred_element_type=F32)
    merged += gate(gb_ref) * jnp.dot(yb_ref[0], wb_ref[...], preferred_element_type=F32)
    merged += gate(gc_ref) * jnp.dot(yc_ref[0], wc_ref[...], preferred_element_type=F32)
    out = x_ref[0] + jnp.dot(merged.astype(BF16), wo_ref[...], preferred_element_type=F32)
    if final_norm:
        ms = jnp.mean(out * out, axis=-1, keepdims=True)
        out = out * lax.rsqrt(ms + RMS_EPS) * fg_ref[...]
    o_ref[0] = out


def _merge(x, u, yb, yc, pool_w, pool_scale, wa, wb, wc, wo, fg, final_norm):
    b, s, d = x.shape
    tm = MERGE_TM
    ublk = lambda name, w: U_COL[name] // w
    gate_blk = U_COL["gates"] // d
    const = lambda shape: pl.BlockSpec(shape, lambda bi, i: (0,) * len(shape))
    row = lambda w, c: pl.BlockSpec((1, tm, w), lambda bi, i: (bi, i, c))
    return pl.pallas_call(
        functools.partial(_merge_kernel, final_norm=final_norm),
        grid=(b, s // tm),
        in_specs=[
            row(d, 0),
            row(POOL_WIDTH, ublk("a_x", POOL_WIDTH)),
            pl.BlockSpec((1, POOL_HALO, POOL_WIDTH),
                         lambda bi, i: (bi, jnp.maximum(i * (tm // POOL_HALO) - 1, 0),
                                        ublk("a_x", POOL_WIDTH))),
            row(POOL_WIDTH, ublk("a_z", POOL_WIDTH)),
            row(SWA_WIDTH, 0),
            row(MOBA_WIDTH, 0),
            row(d, gate_blk), row(d, gate_blk + 1), row(d, gate_blk + 2),
            const((POOL_GROUPS, POOL_GROUP_DIM, POOL_GROUP_DIM)),
            const((1, POOL_WIDTH)),
            const((POOL_WIDTH, d)), const((SWA_WIDTH, d)), const((MOBA_WIDTH, d)),
            const((d, d)),
            const((1, d)),
        ],
        out_specs=row(d, 0),
        out_shape=jax.ShapeDtypeStruct((b, s, d), F32),
        scratch_shapes=[pltpu.VMEM((POOL_HALO + tm, POOL_WIDTH), F32)],
        compiler_params=pltpu.CompilerParams(
            dimension_semantics=("arbitrary", "arbitrary"),
            vmem_limit_bytes=V7X_VMEM_LIMIT_BYTES),
        name="merge",
    )(x, u, u, u, yb, yc, u, u, u, pool_w, pool_scale, wa, wb, wc, wo, fg)


def kernel(x, norm_g, w_in, pool_w, pool_scale, sink_logits, w_proj_a, w_proj_b, w_proj_c,
           w_out, final_norm_g):
    depth = w_in.shape[0]
    assert x.shape[1] % max(INPROJ_TM, SWA_TS, MOBA_TILES * MOBA_TQ, MOBA_SEL_CHUNK, MERGE_TM) == 0
    assert x.shape[2] == D_MODEL and w_in.shape[2] == sum(_REF_SIZES)
    fg = final_norm_g.reshape(1, D_MODEL)
    for l in range(depth):
        w_u = _u_weights(w_in[l])
        sink2 = (sink_logits[l].astype(F32) * LOG2E).reshape(SWA_KV_HEADS, SWA_TILES)
        sink2 = jnp.repeat(sink2, SWA_BLOCK, axis=1)
        wb = _swa_head_major(w_proj_b[l], 0)
        u = _inproj(x, norm_g[l].reshape(1, D_MODEL), w_u)
        yb = _swa(u, sink2)
        yc = _moba(u, _moba_select(u))
        x = _merge(x, u, yb, yc,
                   pool_w[l].astype(BF16), pool_scale[l].reshape(1, POOL_WIDTH),
                   w_proj_a[l].astype(BF16), wb.astype(BF16),
                   w_proj_c[l].astype(BF16), (w_out[l] * GATE_SCALE).astype(BF16), fg,
                   final_norm=(l == depth - 1))
    return x
```

```python
import functools

import numpy as np
import jax
import jax.numpy as jnp
from jax import lax
from jax.experimental import pallas as pl
from jax.experimental.pallas import tpu as pltpu

F32 = jnp.float32
BF16 = jnp.bfloat16

LANES = 128
V7X_VMEM_LIMIT_BYTES = 56 * 1024 * 1024

D_MODEL = 1024
HEAD_DIM = 64
RMS_EPS = 1e-6
NEG_INF = -1e30
LOG2E = 1.4426950408889634
ATTN_SCALE = HEAD_DIM ** -0.5

POOL_WIDTH = 512
POOL_GROUPS = 4
POOL_GROUP_DIM = POOL_WIDTH // POOL_GROUPS
POOL_WINDOWS = (2, 4, 8, 16)
POOL_HALO = 16

SWA_Q_HEADS = 8
SWA_KV_HEADS = 2
SWA_BLOCK = 128
SWA_WIDTH = SWA_Q_HEADS * HEAD_DIM
SWA_KV_WIDTH = SWA_KV_HEADS * HEAD_DIM

MOBA_HEADS = 8
MOBA_BLOCK = 256
MOBA_TOPK = 3
MOBA_WIDTH = MOBA_HEADS * HEAD_DIM

N_BRANCH = 3

_REF_SIZES = (POOL_WIDTH, POOL_WIDTH, SWA_WIDTH, SWA_KV_WIDTH, SWA_KV_WIDTH, SWA_WIDTH,
              MOBA_WIDTH, MOBA_WIDTH, MOBA_WIDTH, MOBA_WIDTH, N_BRANCH * D_MODEL)
_REF_OFF = dict(zip(("a_x", "a_z", "b_q", "b_k", "b_v", "b_z", "c_q", "c_k", "c_v", "c_z", "gates"),
                    np.cumsum((0,) + _REF_SIZES[:-1]).tolist()))

U_COL = dict(a_x=0, a_z=512, b_q=1024, b_z=1536, c_q=2048, c_k=2560, c_v=3072, c_z=3584,
             gates=4096, b_kv=7168)
U_WIDTH = 7168 + 2 * SWA_KV_WIDTH

GATE_SCALE = 0.5


def _swa_head_major(a, axis):
    shape = a.shape
    a = a.reshape(shape[:axis] + (SWA_KV_HEADS, SWA_Q_HEADS // SWA_KV_HEADS, HEAD_DIM)
                  + shape[axis + 1:])
    return jnp.swapaxes(a, axis, axis + 1).reshape(shape)


def _u_weights(w):
    qscale = np.float32(ATTN_SCALE * LOG2E)

    def ref(name, n):
        return w[:, _REF_OFF[name]:_REF_OFF[name] + n]

    parts = [ref("a_x", 2 * POOL_WIDTH),
             _swa_head_major(ref("b_q", SWA_WIDTH), 1) * qscale,
             _swa_head_major(ref("b_z", SWA_WIDTH), 1),
             ref("c_q", MOBA_WIDTH) * qscale,
             ref("c_k", 3 * MOBA_WIDTH),
             ref("gates", N_BRANCH * D_MODEL) * np.float32(GATE_SCALE),
             ref("b_k", 2 * SWA_KV_WIDTH)]
    out = jnp.concatenate(parts, axis=1).astype(BF16)
    assert out.shape[1] == U_WIDTH
    return out


def _sigmoid(z):
    return 0.5 * jnp.tanh(0.5 * z) + 0.5


def _silu(z):
    return z * _sigmoid(z)


_NT = (((1,), (1,)), ((), ()))


INPROJ_TM = 512
INPROJ_CHUNK = 1024


def _inproj_kernel(x_ref, g_ref, w_ref, u_ref):
    x = x_ref[0]
    ms = jnp.mean(x * x, axis=-1, keepdims=True)
    h = (x * lax.rsqrt(ms + RMS_EPS) * g_ref[...]).astype(BF16)
    for c0 in range(0, U_WIDTH, INPROJ_CHUNK):
        c1 = min(c0 + INPROJ_CHUNK, U_WIDTH)
        u_ref[0, :, c0:c1] = jnp.dot(h, w_ref[:, c0:c1],
                                     preferred_element_type=F32).astype(BF16)


def _inproj(x, g, w):
    b, s, d = x.shape
    return pl.pallas_call(
        _inproj_kernel,
        grid=(b, s // INPROJ_TM),
        in_specs=[pl.BlockSpec((1, INPROJ_TM, d), lambda bi, i: (bi, i, 0)),
                  pl.BlockSpec((1, d), lambda bi, i: (0, 0)),
                  pl.BlockSpec((d, U_WIDTH), lambda bi, i: (0, 0))],
        out_specs=pl.BlockSpec((1, INPROJ_TM, U_WIDTH), lambda bi, i: (bi, i, 0)),
        out_shape=jax.ShapeDtypeStruct((b, s, U_WIDTH), BF16),
        compiler_params=pltpu.CompilerParams(
            dimension_semantics=("arbitrary", "arbitrary"),
            vmem_limit_bytes=V7X_VMEM_LIMIT_BYTES),
        name="inproj",
    )(x, g, w)


SWA_TS = 1024
SWA_TILES = SWA_WIDTH // LANES
SWA_LOOKAHEAD = 2
SUM_ROWS = 16


def _swa_kernel(q_ref, z_ref, kv_ref, kvh_ref, sink_ref, o_ref):
    i = pl.program_id(1)
    nq = SWA_TILES * SWA_BLOCK
    lane = lax.broadcasted_iota(jnp.int32, (SWA_BLOCK, LANES), 1)
    half_mask = (lane < HEAD_DIM, lane >= HEAD_DIM)
    r = lax.broadcasted_iota(jnp.int32, (2 * SWA_BLOCK, nq), 0)
    c = lax.broadcasted_iota(jnp.int32, (2 * SWA_BLOCK, nq), 1) & (SWA_BLOCK - 1)
    band = (r > c) & (r <= c + SWA_BLOCK)
    sum_rows = jnp.ones((SUM_ROWS, 2 * SWA_BLOCK), BF16)

    def kv_pair(n):
        kv_prev = kvh_ref[0] if n == 0 else kv_ref[0, (n - 1) * SWA_BLOCK:n * SWA_BLOCK, :]
        return jnp.concatenate([kv_prev, kv_ref[0, n * SWA_BLOCK:(n + 1) * SWA_BLOCK, :]], axis=0)

    def qk(n, half):
        rows = slice(n * SWA_BLOCK, (n + 1) * SWA_BLOCK)
        qs = jnp.concatenate(
            [jnp.where(half_mask[half], q_ref[0, rows, t * LANES:(t + 1) * LANES], 0)
             for t in range(SWA_TILES)], axis=0)
        return lax.dot_general(kv_pair(n)[:, :SWA_KV_WIDTH], qs, _NT,
                               preferred_element_type=F32)

    def softmax_pv(n, half, st):
        valid = band & ((r >= SWA_BLOCK) | (i > 0)) if n == 0 else band
        st = jnp.where(valid, st, NEG_INF)
        sink = sink_ref[half:half + 1, :]
        m = jnp.maximum(jnp.max(st, axis=0, keepdims=True), sink)
        p = jnp.exp2(st - m).astype(BF16)
        vt = kv_pair(n)[:, SWA_KV_WIDTH:].T
        vt_h = jnp.concatenate([vt[half * HEAD_DIM:(half + 1) * HEAD_DIM], sum_rows], axis=0)
        pv = jnp.dot(vt_h, p, preferred_element_type=F32)
        return pv[:HEAD_DIM] / (pv[HEAD_DIM:HEAD_DIM + 1] + jnp.exp2(sink - m))

    def finish(n, halves):
        rows = slice(n * SWA_BLOCK, (n + 1) * SWA_BLOCK)
        ot = jnp.concatenate(halves, axis=0)
        for t in range(SWA_TILES):
            cols = slice(t * LANES, (t + 1) * LANES)
            o = ot[:, cols].T
            z = z_ref[0, rows, cols].astype(F32)
            o_ref[0, rows, cols] = (o * _silu(z)).astype(BF16)

    tiles = [(n, half) for n in range(SWA_TS // SWA_BLOCK) for half in range(SWA_KV_HEADS)]
    pending, halves = {}, []
    for idx in range(len(tiles) + SWA_LOOKAHEAD):
        if idx < len(tiles):
            pending[idx] = qk(*tiles[idx])
        done = idx - SWA_LOOKAHEAD
        if done >= 0:
            n, half = tiles[done]
            halves.append(softmax_pv(n, half, pending.pop(done)))
            if half == SWA_KV_HEADS - 1:
                finish(n, halves)
                halves = []


def _swa(u, sink2):
    b, s, _ = u.shape
    blk = lambda name, w: U_COL[name] // w
    halo_blocks = SWA_TS // SWA_BLOCK
    return pl.pallas_call(
        _swa_kernel,
        grid=(b, s // SWA_TS),
        in_specs=[
            pl.BlockSpec((1, SWA_TS, SWA_WIDTH), lambda bi, i: (bi, i, blk("b_q", SWA_WIDTH))),
            pl.BlockSpec((1, SWA_TS, SWA_WIDTH), lambda bi, i: (bi, i, blk("b_z", SWA_WIDTH))),
            pl.BlockSpec((1, SWA_TS, 2 * SWA_KV_WIDTH),
                         lambda bi, i: (bi, i, blk("b_kv", 2 * SWA_KV_WIDTH))),
            pl.BlockSpec((1, SWA_BLOCK, 2 * SWA_KV_WIDTH),
                         lambda bi, i: (bi, jnp.maximum(i * halo_blocks - 1, 0),
                                        blk("b_kv", 2 * SWA_KV_WIDTH))),
            pl.BlockSpec((SWA_KV_HEADS, SWA_TILES * SWA_BLOCK), lambda bi, i: (0, 0)),
        ],
        out_specs=pl.BlockSpec((1, SWA_TS, SWA_WIDTH), lambda bi, i: (bi, i, 0)),
        out_shape=jax.ShapeDtypeStruct((b, s, SWA_WIDTH), BF16),
        compiler_params=pltpu.CompilerParams(
            dimension_semantics=("arbitrary", "arbitrary"),
            vmem_limit_bytes=V7X_VMEM_LIMIT_BYTES),
        name="swa",
    )(u, u, u, u, sink2)


MOBA_PAIRS = MOBA_WIDTH // LANES
MOBA_SEL_CHUNK = 1024
MOBA_TQ = 512
MOBA_G = MOBA_TQ // MOBA_BLOCK
MOBA_TILES = 4
MOBA_UNROLL = 4
POS_BIG = 1e30
TILE_ORDER = tuple((h, b) for h in range(2) for b in range(MOBA_G))
PREFETCH_AFTER = ((TILE_ORDER[0], TILE_ORDER[1]), (TILE_ORDER[2],), (TILE_ORDER[3],), ())


def _head_split(q):
    lane = lax.broadcasted_iota(jnp.int32, q.shape, 1)
    return (jnp.where(lane < HEAD_DIM, q, 0), jnp.where(lane >= HEAD_DIM, q, 0))


def _moba_select_kernel(q_ref, k_ref, sel_ref, kmean_ref, *, nblk):
    s = k_ref.shape[1]

    def mean_body(j, carry):
        off = pl.multiple_of(j * MOBA_BLOCK, MOBA_BLOCK)
        kj = k_ref[0, pl.ds(off, MOBA_BLOCK), :].astype(F32)
        kmean_ref[pl.ds(j, 1), :] = jnp.sum(kj, axis=0, keepdims=True) * (1.0 / MOBA_BLOCK)
        return carry
    lax.fori_loop(0, nblk, mean_body, 0)

    km = kmean_ref[...]
    km_hi = km.astype(BF16)
    km_lo = (km - km_hi.astype(F32)).astype(BF16)
    ch = MOBA_SEL_CHUNK
    blk_id = lax.broadcasted_iota(jnp.int32, (nblk, ch), 0)
    blk_f = blk_id.astype(F32)
    col = lax.broadcasted_iota(jnp.int32, (nblk, ch), 1)

    def chunk_body(c, carry):
        off = pl.multiple_of(c * ch, ch)
        q_heads = _head_split(q_ref[0, pl.ds(off, ch), :])
        past = blk_id < lax.shift_right_logical(off + col, MOBA_BLOCK.bit_length() - 1)
        for h in range(2):
            g = (lax.dot_general(km_hi, q_heads[h], _NT, preferred_element_type=F32)
                 + lax.dot_general(km_lo, q_heads[h], _NT, preferred_element_type=F32))
            g = jnp.where(past, g, NEG_INF)
            sel = jnp.zeros((nblk, ch), jnp.bool_)
            for _ in range(MOBA_TOPK):
                mx = jnp.max(g, axis=0, keepdims=True)
                first = jnp.min(jnp.where(g == mx, blk_f, float(nblk)), axis=0, keepdims=True)
                pick = blk_f == first
                sel = sel | pick
                g = jnp.where(pick, -jnp.inf, g)
            sel_ref[0, h, :, pl.ds(off, ch)] = jnp.where(sel & past, 1.0, 0.0)
        return carry
    lax.fori_loop(0, s // ch, chunk_body, 0)


def _moba_select(u):
    b, s, _ = u.shape
    nblk = s // MOBA_BLOCK
    blk = lambda name: U_COL[name] // LANES
    return pl.pallas_call(
        functools.partial(_moba_select_kernel, nblk=nblk),
        grid=(b, MOBA_PAIRS),
        in_specs=[pl.BlockSpec((1, s, LANES), lambda bi, p: (bi, 0, blk("c_q") + p)),
                  pl.BlockSpec((1, s, LANES), lambda bi, p: (bi, 0, blk("c_k") + p))],
        out_specs=pl.BlockSpec((1, 2, nblk, s), lambda bi, p: (bi, p, 0, 0)),
        out_shape=jax.ShapeDtypeStruct((b, MOBA_HEADS, nblk, s), F32),
        scratch_shapes=[pltpu.VMEM((nblk, LANES), F32)],
        compiler_params=pltpu.CompilerParams(
            dimension_semantics=("arbitrary", "arbitrary"),
            vmem_limit_bytes=V7X_VMEM_LIMIT_BYTES),
        name="moba_select",
    )(u, u)


def _moba_kernel(q_ref, z_ref, sel_ref, k_ref, v_ref, o_ref, sa_ref, sb_ref):
    gk = MOBA_G * MOBA_BLOCK
    blocks = [slice(b * MOBA_BLOCK, (b + 1) * MOBA_BLOCK) for b in range(MOBA_G)]
    sum_rows = jnp.ones((SUM_ROWS, MOBA_BLOCK), BF16)

    def scores(qh, g, h, rows, sc_ref):
        off = pl.multiple_of(g * gk, gk)
        st = lax.dot_general(k_ref[0, pl.ds(off, gk), :][rows], qh[h], _NT,
                             preferred_element_type=F32)
        sc_ref[h, rows, :] = st
        return jnp.max(st, axis=0, keepdims=True)

    def step(g, sc_ref, maxima, on, mask, state, nxt):
        off = pl.multiple_of(g * gk, gk)
        vt = v_ref[0, pl.ds(off, gk), :].T
        masked, m_new = {}, []
        for h in range(2):
            if mask is not None:
                for b, rows in enumerate(blocks):
                    masked[h, b] = jnp.where(mask[rows], sc_ref[h, rows, :], NEG_INF)
                block_max = [jnp.max(masked[h, b], axis=0, keepdims=True) for b in range(MOBA_G)]
            else:
                block_max = maxima[h]
            mh = state[h][0]
            for mb, o in zip(block_max, on[h]):
                mh = jnp.maximum(mh, jnp.where(o > 0.5, mb, NEG_INF))
            m_new.append(mh)

        pv = [None, None]
        next_maxima = [[None] * MOBA_G for _ in range(2)]
        for idx, (h, b) in enumerate(TILE_ORDER):
            rows = blocks[b]
            tile = masked[h, b] if mask is not None else sc_ref[h, rows, :]
            p = jnp.exp2(tile - jnp.where(on[h][b] > 0.5, m_new[h], POS_BIG)).astype(BF16)
            vt_b = jnp.concatenate([vt[h * HEAD_DIM:(h + 1) * HEAD_DIM, rows], sum_rows], axis=0)
            part = jnp.dot(vt_b, p, preferred_element_type=F32)
            pv[h] = part if pv[h] is None else pv[h] + part
            if nxt is not None:
                for hn, bn in PREFETCH_AFTER[idx]:
                    next_maxima[hn][bn] = scores(nxt[0], nxt[1], hn, blocks[bn], nxt[2])

        new_state = []
        for h in range(2):
            m, l, acc = state[h]
            alpha = jnp.exp2(m - m_new[h])
            new_state.append((m_new[h], alpha * l + pv[h][HEAD_DIM:HEAD_DIM + 1],
                              alpha * acc + pv[h][:HEAD_DIM]))
        return tuple(new_state), next_maxima

    def attend(tile, t, qh, bufs, maxima, nxt_qh):
        cols = slice(tile * MOBA_TQ, (tile + 1) * MOBA_TQ)

        def half_step(g, cur_ref, nxt_ref, state, maxima):
            on = [[sel_ref[0, h, pl.ds(g * MOBA_G + b, 1), cols] for b in range(MOBA_G)]
                  for h in range(2)]
            return step(g, cur_ref, maxima, on, None, state, (qh, g + 1, nxt_ref))

        def run(first, count, state, maxima):
            for k in range(count):
                state, maxima = half_step(first + k, bufs[k % 2], bufs[(k + 1) % 2], state, maxima)
            return state, maxima

        init = tuple((jnp.full((1, MOBA_TQ), NEG_INF, F32),
                      jnp.zeros((1, MOBA_TQ), F32),
                      jnp.zeros((HEAD_DIM, MOBA_TQ), F32)) for _ in range(2))
        state, maxima = lax.fori_loop(
            0, t // MOBA_UNROLL, lambda jj, c: run(MOBA_UNROLL * jj, MOBA_UNROLL, *c), (init, maxima))

        def own_group(sc_ref, free_ref, state):
            kr = lax.broadcasted_iota(jnp.int32, (gk, MOBA_TQ), 0)
            qc = lax.broadcasted_iota(jnp.int32, (gk, MOBA_TQ), 1)
            second_half = lax.broadcasted_iota(jnp.int32, (1, MOBA_TQ), 1) >= MOBA_BLOCK
            ones = jnp.ones((1, MOBA_TQ), F32)
            on = [[jnp.where(second_half, sel_ref[0, h, pl.ds(t * MOBA_G, 1), cols], 1.0), ones]
                  for h in range(2)]
            nxt = None if nxt_qh is None else (nxt_qh, 0, free_ref)
            return step(t, sc_ref, None, on, kr <= qc, state, nxt)

        def finish(left):
            def branch(state, maxima):
                state, _ = run(t - left, left, state, maxima)
                state, next_maxima = own_group(bufs[left % 2], bufs[(left + 1) % 2], state)
                return (state, next_maxima) if nxt_qh is not None else state
            return branch

        lefts = [r for r in range(MOBA_UNROLL) if r % MOBA_TILES == tile]
        if len(lefts) == 1:
            out = finish(lefts[0])(state, maxima)
        else:
            out = lax.switch(lax.rem(t, MOBA_UNROLL) // MOBA_TILES, [finish(r) for r in lefts],
                             state, maxima)
        state, next_maxima = out if nxt_qh is not None else (out, None)

        ot = jnp.concatenate([acc / l for (_, l, acc) in state], axis=0)
        z = z_ref[0, cols, :].astype(F32)
        o_ref[0, cols, :] = (ot.T * _silu(z)).astype(BF16)
        return next_maxima

    t0 = MOBA_TILES * pl.program_id(2)
    qhs = [_head_split(q_ref[0, k * MOBA_TQ:(k + 1) * MOBA_TQ, :])
           for k in range(MOBA_TILES)]
    bufs = (sa_ref, sb_ref)
    maxima = [[scores(qhs[0], 0, h, rows, bufs[0]) for rows in blocks] for h in range(2)]
    for k in range(MOBA_TILES):
        nxt_qh = qhs[k + 1] if k + 1 < MOBA_TILES else None
        maxima = attend(k, t0 + k, qhs[k], bufs, maxima, nxt_qh)
        bufs = (bufs[(k + 1) % 2], bufs[k % 2])


def _moba(u, sel):
    b, s, _ = u.shape
    nblk = s // MOBA_BLOCK
    blk = lambda name: U_COL[name] // LANES
    tq = MOBA_TILES * MOBA_TQ
    return pl.pallas_call(
        _moba_kernel,
        grid=(b, MOBA_PAIRS, s // tq),
        in_specs=[
            pl.BlockSpec((1, tq, LANES), lambda bi, p, t: (bi, t, blk("c_q") + p)),
            pl.BlockSpec((1, tq, LANES), lambda bi, p, t: (bi, t, blk("c_z") + p)),
            pl.BlockSpec((1, 2, nblk, tq), lambda bi, p, t: (bi, p, 0, t)),
            pl.BlockSpec((1, s, LANES), lambda bi, p, t: (bi, 0, blk("c_k") + p)),
            pl.BlockSpec((1, s, LANES), lambda bi, p, t: (bi, 0, blk("c_v") + p)),
        ],
        out_specs=pl.BlockSpec((1, tq, LANES), lambda bi, p, t: (bi, t, p)),
        out_shape=jax.ShapeDtypeStruct((b, s, MOBA_WIDTH), BF16),
        scratch_shapes=[pltpu.VMEM((2, MOBA_G * MOBA_BLOCK, MOBA_TQ), F32)] * 2,
        compiler_params=pltpu.CompilerParams(
            dimension_semantics=("arbitrary", "arbitrary", "arbitrary"),
            vmem_limit_bytes=V7X_VMEM_LIMIT_BYTES),
        name="moba",
    )(u, u, sel, u, u)


MERGE_TM = 1024


def _merge_kernel(x_ref, ax_ref, axh_ref, az_ref, yb_ref, yc_ref, ga_ref, gb_ref, gc_ref,
                  pw_ref, ps_ref, wa_ref, wb_ref, wc_ref, wo_ref, fg_ref, o_ref, buf_ref,
                  *, final_norm):
    i = pl.program_id(1)
    tm = MERGE_TM
    buf_ref[0:POOL_HALO, :] = jnp.where(i > 0, axh_ref[0].astype(F32), 0.0)
    buf_ref[POOL_HALO:POOL_HALO + tm, :] = ax_ref[0].astype(F32)

    tpos = i * tm + lax.broadcasted_iota(jnp.int32, (tm, POOL_GROUP_DIM), 0)
    ya = []
    for gi, w in enumerate(POOL_WINDOWS):
        cols = slice(gi * POOL_GROUP_DIM, (gi + 1) * POOL_GROUP_DIM)
        cur = buf_ref[POOL_HALO:POOL_HALO + tm, cols]
        win = cur
        for sft in range(1, w):
            win = win + buf_ref[POOL_HALO - sft:POOL_HALO - sft + tm, cols]
        cnt = jnp.minimum(tpos + 1, w).astype(F32)
        pooled = (win / cnt - cur).astype(BF16)
        mixed = jnp.dot(pooled, pw_ref[gi], preferred_element_type=F32) * ps_ref[:, cols]
        ya.append((mixed * _silu(az_ref[0, :, cols].astype(F32))).astype(BF16))
    ya = jnp.concatenate(ya, axis=1)

    def gate(g_ref):
        return jnp.tanh(g_ref[0].astype(F32)) + 1.0

    merged = gate(ga_ref) * jnp.dot(ya, wa_ref[...], preferred_element_type=F32)
    merged += gate(gb_ref) * jnp.dot(yb_ref[0], wb_ref[...], preferred_element_type=F32)
    merged += gate(gc_ref) * jnp.dot(yc_ref[0], wc_ref[...], preferred_element_type=F32)
    out = x_ref[0] + jnp.dot(merged.astype(BF16), wo_ref[...], preferred_element_type=F32)
    if final_norm:
        ms = jnp.mean(out * out, axis=-1, keepdims=True)
        out = out * lax.rsqrt(ms + RMS_EPS) * fg_ref[...]
    o_ref[0] = out


def _merge(x, u, yb, yc, pool_w, pool_scale, wa, wb, wc, wo, fg, final_norm):
    b, s, d = x.shape
    tm = MERGE_TM
    ublk = lambda name, w: U_COL[name] // w
    gate_blk = U_COL["gates"] // d
    const = lambda shape: pl.BlockSpec(shape, lambda bi, i: (0,) * len(shape))
    row = lambda w, c: pl.BlockSpec((1, tm, w), lambda bi, i: (bi, i, c))
    return pl.pallas_call(
        functools.partial(_merge_kernel, final_norm=final_norm),
        grid=(b, s // tm),
        in_specs=[
            row(d, 0),
            row(POOL_WIDTH, ublk("a_x", POOL_WIDTH)),
            pl.BlockSpec((1, POOL_HALO, POOL_WIDTH),
                         lambda bi, i: (bi, jnp.maximum(i * (tm // POOL_HALO) - 1, 0),
                                        ublk("a_x", POOL_WIDTH))),
            row(POOL_WIDTH, ublk("a_z", POOL_WIDTH)),
            row(SWA_WIDTH, 0),
            row(MOBA_WIDTH, 0),
            row(d, gate_blk), row(d, gate_blk + 1), row(d, gate_blk + 2),
            const((POOL_GROUPS, POOL_GROUP_DIM, POOL_GROUP_DIM)),
            const((1, POOL_WIDTH)),
            const((POOL_WIDTH, d)), const((SWA_WIDTH, d)), const((MOBA_WIDTH, d)),
            const((d, d)),
            const((1, d)),
        ],
        out_specs=row(d, 0),
        out_shape=jax.ShapeDtypeStruct((b, s, d), F32),
        scratch_shapes=[pltpu.VMEM((POOL_HALO + tm, POOL_WIDTH), F32)],
        compiler_params=pltpu.CompilerParams(
            dimension_semantics=("arbitrary", "arbitrary"),
            vmem_limit_bytes=V7X_VMEM_LIMIT_BYTES),
        name="merge",
    )(x, u, u, u, yb, yc, u, u, u, pool_w, pool_scale, wa, wb, wc, wo, fg)


def kernel(x, norm_g, w_in, pool_w, pool_scale, sink_logits, w_proj_a, w_proj_b, w_proj_c,
           w_out, final_norm_g):
    depth = w_in.shape[0]
    assert x.shape[1] % max(INPROJ_TM, SWA_TS, MOBA_TILES * MOBA_TQ, MOBA_SEL_CHUNK, MERGE_TM) == 0
    assert x.shape[2] == D_MODEL and w_in.shape[2] == sum(_REF_SIZES)
    fg = final_norm_g.reshape(1, D_MODEL)
    for l in range(depth):
        w_u = _u_weights(w_in[l])
        sink2 = (sink_logits[l].astype(F32) * LOG2E).reshape(SWA_KV_HEADS, SWA_TILES)
        sink2 = jnp.repeat(sink2, SWA_BLOCK, axis=1)
        wb = _swa_head_major(w_proj_b[l], 0)
        u = _inproj(x, norm_g[l].reshape(1, D_MODEL), w_u)
        yb = _swa(u, sink2)
        yc = _moba(u, _moba_select(u))
        x = _merge(x, u, yb, yc,
                   pool_w[l].astype(BF16), pool_scale[l].reshape(1, POOL_WIDTH),
                   w_proj_a[l].astype(BF16), wb.astype(BF16),
                   w_proj_c[l].astype(BF16), (w_out[l] * GATE_SCALE).astype(BF16), fg,
                   final_norm=(l == depth - 1))
    return x
```

```python
import functools

import numpy as np
import jax
import jax.numpy as jnp
from jax import lax
from jax.experimental import pallas as pl
from jax.experimental.pallas import tpu as pltpu

F32 = jnp.float32
BF16 = jnp.bfloat16

LANES = 128
V7X_VMEM_LIMIT_BYTES = 56 * 1024 * 1024

D_MODEL = 1024
HEAD_DIM = 64
RMS_EPS = 1e-6
NEG_INF = -1e30
LOG2E = 1.4426950408889634
ATTN_SCALE = HEAD_DIM ** -0.5

POOL_WIDTH = 512
POOL_GROUPS = 4
POOL_GROUP_DIM = POOL_WIDTH // POOL_GROUPS
POOL_WINDOWS = (2, 4, 8, 16)
POOL_HALO = 16

SWA_Q_HEADS = 8
SWA_KV_HEADS = 2
SWA_BLOCK = 128
SWA_WIDTH = SWA_Q_HEADS * HEAD_DIM
SWA_KV_WIDTH = SWA_KV_HEADS * HEAD_DIM

MOBA_HEADS = 8
MOBA_BLOCK = 256
MOBA_TOPK = 3
MOBA_WIDTH = MOBA_HEADS * HEAD_DIM

N_BRANCH = 3

_REF_SIZES = (POOL_WIDTH, POOL_WIDTH, SWA_WIDTH, SWA_KV_WIDTH, SWA_KV_WIDTH, SWA_WIDTH,
              MOBA_WIDTH, MOBA_WIDTH, MOBA_WIDTH, MOBA_WIDTH, N_BRANCH * D_MODEL)
_REF_OFF = dict(zip(("a_x", "a_z", "b_q", "b_k", "b_v", "b_z", "c_q", "c_k", "c_v", "c_z", "gates"),
                    np.cumsum((0,) + _REF_SIZES[:-1]).tolist()))

U_COL = dict(a_x=0, a_z=512, b_q=1024, b_z=1536, c_q=2048, c_k=2560, c_v=3072, c_z=3584,
             gates=4096, b_kv=7168)
U_WIDTH = 7168 + 2 * SWA_KV_WIDTH

GATE_SCALE = 0.5


def _swa_head_major(a, axis):
    shape = a.shape
    a = a.reshape(shape[:axis] + (SWA_KV_HEADS, SWA_Q_HEADS // SWA_KV_HEADS, HEAD_DIM)
                  + shape[axis + 1:])
    return jnp.swapaxes(a, axis, axis + 1).reshape(shape)


def _u_weights(w):
    qscale = np.float32(ATTN_SCALE * LOG2E)

    def ref(name, n):
        return w[..., _REF_OFF[name]:_REF_OFF[name] + n]

    parts = [ref("a_x", 2 * POOL_WIDTH),
             _swa_head_major(ref("b_q", SWA_WIDTH), 2) * qscale,
             _swa_head_major(ref("b_z", SWA_WIDTH), 2),
             ref("c_q", MOBA_WIDTH) * qscale,
             ref("c_k", 3 * MOBA_WIDTH),
             ref("gates", N_BRANCH * D_MODEL) * np.float32(GATE_SCALE),
             ref("b_k", 2 * SWA_KV_WIDTH)]
    out = jnp.concatenate(parts, axis=-1).astype(BF16)
    assert out.shape[-1] == U_WIDTH
    return out


def _sigmoid(z):
    return 0.5 * jnp.tanh(0.5 * z) + 0.5


def _silu(z):
    return z * _sigmoid(z)


_NT = (((1,), (1,)), ((), ()))


INPROJ_TM = 512
INPROJ_CHUNK = 1024


def _inproj_kernel(x_ref, g_ref, w_ref, u_ref):
    x = x_ref[0]
    ms = jnp.mean(x * x, axis=-1, keepdims=True)
    h = (x * lax.rsqrt(ms + RMS_EPS) * g_ref[...]).astype(BF16)
    for c0 in range(0, U_WIDTH, INPROJ_CHUNK):
        c1 = min(c0 + INPROJ_CHUNK, U_WIDTH)
        u_ref[0, :, c0:c1] = jnp.dot(h, w_ref[0, :, c0:c1],
                                     preferred_element_type=F32).astype(BF16)


def _inproj(x, g, w, layer):
    b, s, d = x.shape
    return pl.pallas_call(
        _inproj_kernel,
        grid=(b, s // INPROJ_TM),
        in_specs=[pl.BlockSpec((1, INPROJ_TM, d), lambda bi, i: (bi, i, 0)),
                  pl.BlockSpec((1, d), lambda bi, i: (0, 0)),
                  pl.BlockSpec((1, d, U_WIDTH), lambda bi, i: (layer, 0, 0))],
        out_specs=pl.BlockSpec((1, INPROJ_TM, U_WIDTH), lambda bi, i: (bi, i, 0)),
        out_shape=jax.ShapeDtypeStruct((b, s, U_WIDTH), BF16),
        compiler_params=pltpu.CompilerParams(
            dimension_semantics=("arbitrary", "arbitrary"),
            vmem_limit_bytes=V7X_VMEM_LIMIT_BYTES),
        name="inproj",
    )(x, g, w)


SWA_TS = 1024
SWA_TILES = SWA_WIDTH // LANES
SWA_LOOKAHEAD = 2
SUM_ROWS = 16


def _swa_kernel(q_ref, z_ref, kv_ref, kvh_ref, sink_ref, o_ref):
    i = pl.program_id(1)
    nq = SWA_TILES * SWA_BLOCK
    lane = lax.broadcasted_iota(jnp.int32, (SWA_BLOCK, LANES), 1)
    half_mask = (lane < HEAD_DIM, lane >= HEAD_DIM)
    r = lax.broadcasted_iota(jnp.int32, (2 * SWA_BLOCK, nq), 0)
    c = lax.broadcasted_iota(jnp.int32, (2 * SWA_BLOCK, nq), 1) & (SWA_BLOCK - 1)
    band = (r > c) & (r <= c + SWA_BLOCK)
    sum_rows = jnp.ones((SUM_ROWS, 2 * SWA_BLOCK), BF16)

    def kv_pair(n):
        kv_prev = kvh_ref[0] if n == 0 else kv_ref[0, (n - 1) * SWA_BLOCK:n * SWA_BLOCK, :]
        return jnp.concatenate([kv_prev, kv_ref[0, n * SWA_BLOCK:(n + 1) * SWA_BLOCK, :]], axis=0)

    def qk(n, half):
        rows = slice(n * SWA_BLOCK, (n + 1) * SWA_BLOCK)
        qs = jnp.concatenate(
            [jnp.where(half_mask[half], q_ref[0, rows, t * LANES:(t + 1) * LANES], 0)
             for t in range(SWA_TILES)], axis=0)
        return lax.dot_general(kv_pair(n)[:, :SWA_KV_WIDTH], qs, _NT,
                               preferred_element_type=F32)

    def softmax_pv(n, half, st):
        valid = band & ((r >= SWA_BLOCK) | (i > 0)) if n == 0 else band
        st = jnp.where(valid, st, NEG_INF)
        sink = sink_ref[half:half + 1, :]
        m = jnp.maximum(jnp.max(st, axis=0, keepdims=True), sink)
        p = jnp.exp2(st - m).astype(BF16)
        vt = kv_pair(n)[:, SWA_KV_WIDTH:].T
        vt_h = jnp.concatenate([vt[half * HEAD_DIM:(half + 1) * HEAD_DIM], sum_rows], axis=0)
        pv = jnp.dot(vt_h, p, preferred_element_type=F32)
        return pv[:HEAD_DIM] / (pv[HEAD_DIM:HEAD_DIM + 1] + jnp.exp2(sink - m))

    def finish(n, halves):
        rows = slice(n * SWA_BLOCK, (n + 1) * SWA_BLOCK)
        ot = jnp.concatenate(halves, axis=0)
        for t in range(SWA_TILES):
            cols = slice(t * LANES, (t + 1) * LANES)
            o = ot[:, cols].T
            z = z_ref[0, rows, cols].astype(F32)
            o_ref[0, rows, cols] = (o * _silu(z)).astype(BF16)

    tiles = [(n, half) for n in range(SWA_TS // SWA_BLOCK) for half in range(SWA_KV_HEADS)]
    pending, halves = {}, []
    for idx in range(len(tiles) + SWA_LOOKAHEAD):
        if idx < len(tiles):
            pending[idx] = qk(*tiles[idx])
        done = idx - SWA_LOOKAHEAD
        if done >= 0:
            n, half = tiles[done]
            halves.append(softmax_pv(n, half, pending.pop(done)))
            if half == SWA_KV_HEADS - 1:
                finish(n, halves)
                halves = []


def _swa(u, sink2):
    b, s, _ = u.shape
    blk = lambda name, w: U_COL[name] // w
    halo_blocks = SWA_TS // SWA_BLOCK
    return pl.pallas_call(
        _swa_kernel,
        grid=(b, s // SWA_TS),
        in_specs=[
            pl.BlockSpec((1, SWA_TS, SWA_WIDTH), lambda bi, i: (bi, i, blk("b_q", SWA_WIDTH))),
            pl.BlockSpec((1, SWA_TS, SWA_WIDTH), lambda bi, i: (bi, i, blk("b_z", SWA_WIDTH))),
            pl.BlockSpec((1, SWA_TS, 2 * SWA_KV_WIDTH),
                         lambda bi, i: (bi, i, blk("b_kv", 2 * SWA_KV_WIDTH))),
            pl.BlockSpec((1, SWA_BLOCK, 2 * SWA_KV_WIDTH),
                         lambda bi, i: (bi, jnp.maximum(i * halo_blocks - 1, 0),
                                        blk("b_kv", 2 * SWA_KV_WIDTH))),
            pl.BlockSpec((SWA_KV_HEADS, SWA_TILES * SWA_BLOCK), lambda bi, i: (0, 0)),
        ],
        out_specs=pl.BlockSpec((1, SWA_TS, SWA_WIDTH), lambda bi, i: (bi, i, 0)),
        out_shape=jax.ShapeDtypeStruct((b, s, SWA_WIDTH), BF16),
        compiler_params=pltpu.CompilerParams(
            dimension_semantics=("arbitrary", "arbitrary"),
            vmem_limit_bytes=V7X_VMEM_LIMIT_BYTES),
        name="swa",
    )(u, u, u, u, sink2)


MOBA_PAIRS = MOBA_WIDTH // LANES
MOBA_SEL_CHUNK = 2048
MOBA_TQ = 512
MOBA_G = MOBA_TQ // MOBA_BLOCK
MOBA_TILES = 4
MOBA_UNROLL = 4
POS_BIG = 1e30
TILE_ORDER = tuple((h, b) for h in range(2) for b in range(MOBA_G))
PREFETCH_AFTER = ((TILE_ORDER[0], TILE_ORDER[1]), (TILE_ORDER[2],), (TILE_ORDER[3],), ())


def _head_split(q):
    lane = lax.broadcasted_iota(jnp.int32, q.shape, 1)
    return (jnp.where(lane < HEAD_DIM, q, 0), jnp.where(lane >= HEAD_DIM, q, 0))


def _moba_select_kernel(q_ref, k_ref, sel_ref, kmean_ref, *, nblk):
    s = k_ref.shape[1]

    def mean_body(j, carry):
        off = pl.multiple_of(j * MOBA_BLOCK, MOBA_BLOCK)
        kj = k_ref[0, pl.ds(off, MOBA_BLOCK), :].astype(F32)
        kmean_ref[pl.ds(j, 1), :] = jnp.sum(kj, axis=0, keepdims=True) * (1.0 / MOBA_BLOCK)
        return carry
    lax.fori_loop(0, nblk, mean_body, 0)

    km = kmean_ref[...]
    km_hi = km.astype(BF16)
    km_lo = (km - km_hi.astype(F32)).astype(BF16)
    ch = MOBA_SEL_CHUNK
    blk_id = lax.broadcasted_iota(jnp.int32, (nblk, ch), 0)
    blk_f = blk_id.astype(F32)
    col = lax.broadcasted_iota(jnp.int32, (nblk, ch), 1)

    def chunk_body(c, carry):
        off = pl.multiple_of(c * ch, ch)
        q_heads = _head_split(q_ref[0, pl.ds(off, ch), :])
        past = blk_id < lax.shift_right_logical(off + col, MOBA_BLOCK.bit_length() - 1)
        for h in range(2):
            g = (lax.dot_general(km_hi, q_heads[h], _NT, preferred_element_type=F32)
                 + lax.dot_general(km_lo, q_heads[h], _NT, preferred_element_type=F32))
            g = jnp.where(past, g, NEG_INF)
            sel = jnp.zeros((nblk, ch), jnp.bool_)
            for _ in range(MOBA_TOPK):
                mx = jnp.max(g, axis=0, keepdims=True)
                first = jnp.min(jnp.where(g == mx, blk_f, float(nblk)), axis=0, keepdims=True)
                pick = blk_f == first
                sel = sel | pick
                g = jnp.where(pick, -jnp.inf, g)
            sel_ref[0, h, :, pl.ds(off, ch)] = jnp.where(sel & past, 1.0, 0.0)
        return carry
    lax.fori_loop(0, s // ch, chunk_body, 0)


def _moba_select(u):
    b, s, _ = u.shape
    nblk = s // MOBA_BLOCK
    blk = lambda name: U_COL[name] // LANES
    return pl.pallas_call(
        functools.partial(_moba_select_kernel, nblk=nblk),
        grid=(b, MOBA_PAIRS),
        in_specs=[pl.BlockSpec((1, s, LANES), lambda bi, p: (bi, 0, blk("c_q") + p)),
                  pl.BlockSpec((1, s, LANES), lambda bi, p: (bi, 0, blk("c_k") + p))],
        out_specs=pl.BlockSpec((1, 2, nblk, s), lambda bi, p: (bi, p, 0, 0)),
        out_shape=jax.ShapeDtypeStruct((b, MOBA_HEADS, nblk, s), F32),
        scratch_shapes=[pltpu.VMEM((nblk, LANES), F32)],
        compiler_params=pltpu.CompilerParams(
            dimension_semantics=("arbitrary", "arbitrary"),
            vmem_limit_bytes=V7X_VMEM_LIMIT_BYTES),
        name="moba_select",
    )(u, u)


def _moba_kernel(q_ref, z_ref, sel_ref, k_ref, v_ref, o_ref, sa_ref, sb_ref):
    gk = MOBA_G * MOBA_BLOCK
    blocks = [slice(b * MOBA_BLOCK, (b + 1) * MOBA_BLOCK) for b in range(MOBA_G)]
    sum_rows = jnp.ones((SUM_ROWS, MOBA_BLOCK), BF16)

    def scores(qh, g, h, rows, sc_ref):
        off = pl.multiple_of(g * gk, gk)
        st = lax.dot_general(k_ref[0, pl.ds(off, gk), :][rows], qh[h], _NT,
                             preferred_element_type=F32)
        sc_ref[h, rows, :] = st
        return jnp.max(st, axis=0, keepdims=True)

    def step(g, sc_ref, maxima, on, mask, state, nxt):
        off = pl.multiple_of(g * gk, gk)
        vt = v_ref[0, pl.ds(off, gk), :].T
        masked, m_new = {}, []
        for h in range(2):
            if mask is not None:
                for b, rows in enumerate(blocks):
                    masked[h, b] = jnp.where(mask[rows], sc_ref[h, rows, :], NEG_INF)
                block_max = [jnp.max(masked[h, b], axis=0, keepdims=True) for b in range(MOBA_G)]
            else:
                block_max = maxima[h]
            mh = state[h][0]
            for mb, o in zip(block_max, on[h]):
                mh = jnp.maximum(mh, jnp.where(o > 0.5, mb, NEG_INF))
            m_new.append(mh)

        pv = [None, None]
        next_maxima = [[None] * MOBA_G for _ in range(2)]
        for idx, (h, b) in enumerate(TILE_ORDER):
            rows = blocks[b]
            tile = masked[h, b] if mask is not None else sc_ref[h, rows, :]
            p = jnp.exp2(tile - jnp.where(on[h][b] > 0.5, m_new[h], POS_BIG)).astype(BF16)
            vt_b = jnp.concatenate([vt[h * HEAD_DIM:(h + 1) * HEAD_DIM, rows], sum_rows], axis=0)
            part = jnp.dot(vt_b, p, preferred_element_type=F32)
            pv[h] = part if pv[h] is None else pv[h] + part
            if nxt is not None:
                for hn, bn in PREFETCH_AFTER[idx]:
                    next_maxima[hn][bn] = scores(nxt[0], nxt[1], hn, blocks[bn], nxt[2])

        new_state = []
        for h in range(2):
            m, l, acc = state[h]
            alpha = jnp.exp2(m - m_new[h])
            new_state.append((m_new[h], alpha * l + pv[h][HEAD_DIM:HEAD_DIM + 1],
                              alpha * acc + pv[h][:HEAD_DIM]))
        return tuple(new_state), next_maxima

    def attend(tile, t, qh, bufs, maxima, nxt_qh):
        cols = slice(tile * MOBA_TQ, (tile + 1) * MOBA_TQ)

        def half_step(g, cur_ref, nxt_ref, state, maxima):
            on = [[sel_ref[0, h, pl.ds(g * MOBA_G + b, 1), cols] for b in range(MOBA_G)]
                  for h in range(2)]
            return step(g, cur_ref, maxima, on, None, state, (qh, g + 1, nxt_ref))

        def run(first, count, state, maxima):
            for k in range(count):
                state, maxima = half_step(first + k, bufs[k % 2], bufs[(k + 1) % 2], state, maxima)
            return state, maxima

        init = tuple((jnp.full((1, MOBA_TQ), NEG_INF, F32),
                      jnp.zeros((1, MOBA_TQ), F32),
                      jnp.zeros((HEAD_DIM, MOBA_TQ), F32)) for _ in range(2))
        state, maxima = lax.fori_loop(
            0, t // MOBA_UNROLL, lambda jj, c: run(MOBA_UNROLL * jj, MOBA_UNROLL, *c), (init, maxima))

        def own_group(sc_ref, free_ref, state):
            kr = lax.broadcasted_iota(jnp.int32, (gk, MOBA_TQ), 0)
            qc = lax.broadcasted_iota(jnp.int32, (gk, MOBA_TQ), 1)
            second_half = lax.broadcasted_iota(jnp.int32, (1, MOBA_TQ), 1) >= MOBA_BLOCK
            ones = jnp.ones((1, MOBA_TQ), F32)
            on = [[jnp.where(second_half, sel_ref[0, h, pl.ds(t * MOBA_G, 1), cols], 1.0), ones]
                  for h in range(2)]
            nxt = None if nxt_qh is None else (nxt_qh, 0, free_ref)
            return step(t, sc_ref, None, on, kr <= qc, state, nxt)

        def finish(left):
            def branch(state, maxima):
                state, _ = run(t - left, left, state, maxima)
                state, next_maxima = own_group(bufs[left % 2], bufs[(left + 1) % 2], state)
                return (state, next_maxima) if nxt_qh is not None else state
            return branch

        lefts = [r for r in range(MOBA_UNROLL) if r % MOBA_TILES == tile]
        if len(lefts) == 1:
            out = finish(lefts[0])(state, maxima)
        else:
            out = lax.switch(lax.rem(t, MOBA_UNROLL) // MOBA_TILES, [finish(r) for r in lefts],
                             state, maxima)
        state, next_maxima = out if nxt_qh is not None else (out, None)

        ot = jnp.concatenate([acc / l for (_, l, acc) in state], axis=0)
        z = z_ref[0, cols, :].astype(F32)
        o_ref[0, cols, :] = (ot.T * _silu(z)).astype(BF16)
        return next_maxima

    t0 = MOBA_TILES * pl.program_id(2)
    qhs = [_head_split(q_ref[0, k * MOBA_TQ:(k + 1) * MOBA_TQ, :])
           for k in range(MOBA_TILES)]
    bufs = (sa_ref, sb_ref)
    maxima = [[scores(qhs[0], 0, h, rows, bufs[0]) for rows in blocks] for h in range(2)]
    for k in range(MOBA_TILES):
        nxt_qh = qhs[k + 1] if k + 1 < MOBA_TILES else None
        maxima = attend(k, t0 + k, qhs[k], bufs, maxima, nxt_qh)
        bufs = (bufs[(k + 1) % 2], bufs[k % 2])


def _moba(u, sel):
    b, s, _ = u.shape
    nblk = s // MOBA_BLOCK
    blk = lambda name: U_COL[name] // LANES
    tq = MOBA_TILES * MOBA_TQ
    return pl.pallas_call(
        _moba_kernel,
        grid=(b, MOBA_PAIRS, s // tq),
        in_specs=[
            pl.BlockSpec((1, tq, LANES), lambda bi, p, t: (bi, t, blk("c_q") + p)),
            pl.BlockSpec((1, tq, LANES), lambda bi, p, t: (bi, t, blk("c_z") + p)),
            pl.BlockSpec((1, 2, nblk, tq), lambda bi, p, t: (bi, p, 0, t)),
            pl.BlockSpec((1, s, LANES), lambda bi, p, t: (bi, 0, blk("c_k") + p)),
            pl.BlockSpec((1, s, LANES), lambda bi, p, t: (bi, 0, blk("c_v") + p)),
        ],
        out_specs=pl.BlockSpec((1, tq, LANES), lambda bi, p, t: (bi, t, p)),
        out_shape=jax.ShapeDtypeStruct((b, s, MOBA_WIDTH), BF16),
        scratch_shapes=[pltpu.VMEM((2, MOBA_G * MOBA_BLOCK, MOBA_TQ), F32)] * 2,
        compiler_params=pltpu.CompilerParams(
            dimension_semantics=("arbitrary", "arbitrary", "arbitrary"),
            vmem_limit_bytes=V7X_VMEM_LIMIT_BYTES),
        name="moba",
    )(u, u, sel, u, u)


MERGE_TM = 1024


def _merge_kernel(x_ref, ax_ref, axh_ref, az_ref, yb_ref, yc_ref, ga_ref, gb_ref, gc_ref,
                  pw_ref, ps_ref, wa_ref, wb_ref, wc_ref, wo_ref, fg_ref, o_ref, buf_ref,
                  *, final_norm):
    i = pl.program_id(1)
    tm = MERGE_TM
    buf_ref[0:POOL_HALO, :] = jnp.where(i > 0, axh_ref[0].astype(F32), 0.0)
    buf_ref[POOL_HALO:POOL_HALO + tm, :] = ax_ref[0].astype(F32)

    tpos = i * tm + lax.broadcasted_iota(jnp.int32, (tm, POOL_GROUP_DIM), 0)
    ya = []
    for gi, w in enumerate(POOL_WINDOWS):
        cols = slice(gi * POOL_GROUP_DIM, (gi + 1) * POOL_GROUP_DIM)
        cur = buf_ref[POOL_HALO:POOL_HALO + tm, cols]
        win = cur
        for sft in range(1, w):
            win = win + buf_ref[POOL_HALO - sft:POOL_HALO - sft + tm, cols]
        cnt = jnp.minimum(tpos + 1, w).astype(F32)
        pooled = (win / cnt - cur).astype(BF16)
        mixed = jnp.dot(pooled, pw_ref[gi], preferred_element_type=F32) * ps_ref[:, cols]
        ya.append((mixed * _silu(az_ref[0, :, cols].astype(F32))).astype(BF16))
    ya = jnp.concatenate(ya, axis=1)

    def gate(g_ref):
        return jnp.tanh(g_ref[0].astype(F32)) + 1.0

    merged = gate(ga_ref) * jnp.dot(ya, wa_ref[...], preferred_element_type=F32)
    merged += gate(gb_ref) * jnp.dot(yb_ref[0], wb_ref[...], preferred_element_type=F32)
    merged += gate(gc_ref) * jnp.dot(yc_ref[0], wc_ref[...], preferred_element_type=F32)
    out = x_ref[0] + jnp.dot(merged.astype(BF16), wo_ref[...], preferred_element_type=F32)
    if final_norm:
        ms = jnp.mean(out * out, axis=-1, keepdims=True)
        out = out * lax.rsqrt(ms + RMS_EPS) * fg_ref[...]
    o_ref[0] = out


def _merge(x, u, yb, yc, pool_w, pool_scale, wa, wb, wc, wo, fg, final_norm):
    b, s, d = x.shape
    tm = MERGE_TM
    ublk = lambda name, w: U_COL[name] // w
    gate_blk = U_COL["gates"] // d
    const = lambda shape: pl.BlockSpec(shape, lambda bi, i: (0,) * len(shape))
    row = lambda w, c: pl.BlockSpec((1, tm, w), lambda bi, i: (bi, i, c))
    return pl.pallas_call(
        functools.partial(_merge_kernel, final_norm=final_norm),
        grid=(b, s // tm),
        in_specs=[
            row(d, 0),
            row(POOL_WIDTH, ublk("a_x", POOL_WIDTH)),
            pl.BlockSpec((1, POOL_HALO, POOL_WIDTH),
                         lambda bi, i: (bi, jnp.maximum(i * (tm // POOL_HALO) - 1, 0),
                                        ublk("a_x", POOL_WIDTH))),
            row(POOL_WIDTH, ublk("a_z", POOL_WIDTH)),
            row(SWA_WIDTH, 0),
            row(MOBA_WIDTH, 0),
            row(d, gate_blk), row(d, gate_blk + 1), row(d, gate_blk + 2),
            const((POOL_GROUPS, POOL_GROUP_DIM, POOL_GROUP_DIM)),
            const((1, POOL_WIDTH)),
            const((POOL_WIDTH, d)), const((SWA_WIDTH, d)), const((MOBA_WIDTH, d)),
            const((d, d)),
            const((1, d)),
        ],
        out_specs=row(d, 0),
        out_shape=jax.ShapeDtypeStruct((b, s, d), F32),
        scratch_shapes=[pltpu.VMEM((POOL_HALO + tm, POOL_WIDTH), F32)],
        compiler_params=pltpu.CompilerParams(
            dimension_semantics=("arbitrary", "arbitrary"),
            vmem_limit_bytes=V7X_VMEM_LIMIT_BYTES),
        name="merge",
    )(x, u, u, u, yb, yc, u, u, u, pool_w, pool_scale, wa, wb, wc, wo, fg)


def kernel(x, norm_g, w_in, pool_w, pool_scale, sink_logits, w_proj_a, w_proj_b, w_proj_c,
           w_out, final_norm_g):
    depth = w_in.shape[0]
    assert x.shape[1] % max(INPROJ_TM, SWA_TS, MOBA_TILES * MOBA_TQ, MOBA_SEL_CHUNK, MERGE_TM) == 0
    assert x.shape[2] == D_MODEL and w_in.shape[2] == sum(_REF_SIZES)
    fg = final_norm_g.reshape(1, D_MODEL)
    w_u = _u_weights(w_in)
    for l in range(depth):
        sink2 = (sink_logits[l].astype(F32) * LOG2E).reshape(SWA_KV_HEADS, SWA_TILES)
        sink2 = jnp.repeat(sink2, SWA_BLOCK, axis=1)
        wb = _swa_head_major(w_proj_b[l], 0)
        u = _inproj(x, norm_g[l].reshape(1, D_MODEL), w_u, l)
        yb = _swa(u, sink2)
        yc = _moba(u, _moba_select(u))
        x = _merge(x, u, yb, yc,
                   pool_w[l].astype(BF16), pool_scale[l].reshape(1, POOL_WIDTH),
                   w_proj_a[l].astype(BF16), wb.astype(BF16),
                   w_proj_c[l].astype(BF16), (w_out[l] * GATE_SCALE).astype(BF16), fg,
                   final_norm=(l == depth - 1))
    return x
```

```python
import functools

import numpy as np
import jax
import jax.numpy as jnp
from jax import lax
from jax.experimental import pallas as pl
from jax.experimental.pallas import tpu as pltpu

F32 = jnp.float32
BF16 = jnp.bfloat16

LANES = 128
V7X_VMEM_LIMIT_BYTES = 56 * 1024 * 1024

D_MODEL = 1024
HEAD_DIM = 64
RMS_EPS = 1e-6
NEG_INF = -1e30
LOG2E = 1.4426950408889634
ATTN_SCALE = HEAD_DIM ** -0.5

POOL_WIDTH = 512
POOL_GROUPS = 4
POOL_GROUP_DIM = POOL_WIDTH // POOL_GROUPS
POOL_WINDOWS = (2, 4, 8, 16)
POOL_HALO = 128

SWA_Q_HEADS = 8
SWA_KV_HEADS = 2
SWA_BLOCK = 128
SWA_WIDTH = SWA_Q_HEADS * HEAD_DIM
SWA_KV_WIDTH = SWA_KV_HEADS * HEAD_DIM

MOBA_HEADS = 8
MOBA_BLOCK = 256
MOBA_TOPK = 3
MOBA_WIDTH = MOBA_HEADS * HEAD_DIM

N_BRANCH = 3

_REF_SIZES = (POOL_WIDTH, POOL_WIDTH, SWA_WIDTH, SWA_KV_WIDTH, SWA_KV_WIDTH, SWA_WIDTH,
              MOBA_WIDTH, MOBA_WIDTH, MOBA_WIDTH, MOBA_WIDTH, N_BRANCH * D_MODEL)
_REF_OFF = dict(zip(("a_x", "a_z", "b_q", "b_k", "b_v", "b_z", "c_q", "c_k", "c_v", "c_z", "gates"),
                    np.cumsum((0,) + _REF_SIZES[:-1]).tolist()))

U_COL = dict(a_x=0, a_z=512, b_q=1024, b_z=1536, c_q=2048, c_k=2560, c_v=3072, c_z=3584,
             gates=4096, b_kv=7168)
U_WIDTH = 7168 + 2 * SWA_KV_WIDTH

GATE_SCALE = 0.5


def _swa_head_major(a, axis):
    shape = a.shape
    a = a.reshape(shape[:axis] + (SWA_KV_HEADS, SWA_Q_HEADS // SWA_KV_HEADS, HEAD_DIM)
                  + shape[axis + 1:])
    return jnp.swapaxes(a, axis, axis + 1).reshape(shape)


def _u_weights(w):
    qscale = np.float32(ATTN_SCALE * LOG2E)

    def ref(name, n):
        return w[..., _REF_OFF[name]:_REF_OFF[name] + n]

    parts = [ref("a_x", 2 * POOL_WIDTH),
             _swa_head_major(ref("b_q", SWA_WIDTH), 2) * qscale,
             _swa_head_major(ref("b_z", SWA_WIDTH), 2),
             ref("c_q", MOBA_WIDTH) * qscale,
             ref("c_k", 3 * MOBA_WIDTH),
             ref("gates", N_BRANCH * D_MODEL) * np.float32(GATE_SCALE),
             ref("b_k", 2 * SWA_KV_WIDTH)]
    out = jnp.concatenate(parts, axis=-1).astype(BF16)
    assert out.shape[-1] == U_WIDTH
    return out


def _sigmoid(z):
    return 0.5 * jnp.tanh(0.5 * z) + 0.5


def _silu(z):
    return z * _sigmoid(z)


_NT = (((1,), (1,)), ((), ()))


INPROJ_TM = 512
INPROJ_CHUNK = 1024


def _inproj_kernel(x_ref, g_ref, w_ref, u_ref):
    x = x_ref[0]
    ms = jnp.mean(x * x, axis=-1, keepdims=True)
    h = (x * lax.rsqrt(ms + RMS_EPS) * g_ref[...]).astype(BF16)
    for c0 in range(0, U_WIDTH, INPROJ_CHUNK):
        c1 = min(c0 + INPROJ_CHUNK, U_WIDTH)
        u_ref[0, :, c0:c1] = jnp.dot(h, w_ref[0, :, c0:c1],
                                     preferred_element_type=F32).astype(BF16)


def _inproj(x, g, w, layer):
    b, s, d = x.shape
    return pl.pallas_call(
        _inproj_kernel,
        grid=(b, s // INPROJ_TM),
        in_specs=[pl.BlockSpec((1, INPROJ_TM, d), lambda bi, i: (bi, i, 0)),
                  pl.BlockSpec((1, d), lambda bi, i: (0, 0)),
                  pl.BlockSpec((1, d, U_WIDTH), lambda bi, i: (layer, 0, 0))],
        out_specs=pl.BlockSpec((1, INPROJ_TM, U_WIDTH), lambda bi, i: (bi, i, 0)),
        out_shape=jax.ShapeDtypeStruct((b, s, U_WIDTH), BF16),
        compiler_params=pltpu.CompilerParams(
            dimension_semantics=("arbitrary", "arbitrary"),
            vmem_limit_bytes=V7X_VMEM_LIMIT_BYTES),
        name="inproj",
    )(x, g, w)


SWA_TS = 1024
SWA_TILES = SWA_WIDTH // LANES
SWA_LOOKAHEAD = 2
SUM_ROWS = 16


def _swa_kernel(q_ref, z_ref, kv_ref, kvh_ref, sink_ref, o_ref):
    i = pl.program_id(1)
    nq = SWA_TILES * SWA_BLOCK
    lane = lax.broadcasted_iota(jnp.int32, (SWA_BLOCK, LANES), 1)
    half_mask = (lane < HEAD_DIM, lane >= HEAD_DIM)
    r = lax.broadcasted_iota(jnp.int32, (2 * SWA_BLOCK, nq), 0)
    c = lax.broadcasted_iota(jnp.int32, (2 * SWA_BLOCK, nq), 1) & (SWA_BLOCK - 1)
    band = (r > c) & (r <= c + SWA_BLOCK)
    sum_rows = jnp.ones((SUM_ROWS, 2 * SWA_BLOCK), BF16)

    def kv_pair(n):
        kv_prev = kvh_ref[0] if n == 0 else kv_ref[0, (n - 1) * SWA_BLOCK:n * SWA_BLOCK, :]
        return jnp.concatenate([kv_prev, kv_ref[0, n * SWA_BLOCK:(n + 1) * SWA_BLOCK, :]], axis=0)

    def qk(n, half):
        rows = slice(n * SWA_BLOCK, (n + 1) * SWA_BLOCK)
        qs = jnp.concatenate(
            [jnp.where(half_mask[half], q_ref[0, rows, t * LANES:(t + 1) * LANES], 0)
             for t in range(SWA_TILES)], axis=0)
        return lax.dot_general(kv_pair(n)[:, :SWA_KV_WIDTH], qs, _NT,
                               preferred_element_type=F32)

    def softmax_pv(n, half, st):
        valid = band & ((r >= SWA_BLOCK) | (i > 0)) if n == 0 else band
        st = jnp.where(valid, st, NEG_INF)
        sink = sink_ref[half:half + 1, :]
        m = jnp.maximum(jnp.max(st, axis=0, keepdims=True), sink)
        p = jnp.exp2(st - m).astype(BF16)
        vt = kv_pair(n)[:, SWA_KV_WIDTH:].T
        vt_h = jnp.concatenate([vt[half * HEAD_DIM:(half + 1) * HEAD_DIM], sum_rows], axis=0)
        pv = jnp.dot(vt_h, p, preferred_element_type=F32)
        return pv[:HEAD_DIM] / (pv[HEAD_DIM:HEAD_DIM + 1] + jnp.exp2(sink - m))

    def finish(n, halves):
        rows = slice(n * SWA_BLOCK, (n + 1) * SWA_BLOCK)
        ot = jnp.concatenate(halves, axis=0)
        for t in range(SWA_TILES):
            cols = slice(t * LANES, (t + 1) * LANES)
            o = ot[:, cols].T
            z = z_ref[0, rows, cols].astype(F32)
            o_ref[0, rows, cols] = (o * _silu(z)).astype(BF16)

    tiles = [(n, half) for n in range(SWA_TS // SWA_BLOCK) for half in range(SWA_KV_HEADS)]
    pending, halves = {}, []
    for idx in range(len(tiles) + SWA_LOOKAHEAD):
        if idx < len(tiles):
            pending[idx] = qk(*tiles[idx])
        done = idx - SWA_LOOKAHEAD
        if done >= 0:
            n, half = tiles[done]
            halves.append(softmax_pv(n, half, pending.pop(done)))
            if half == SWA_KV_HEADS - 1:
                finish(n, halves)
                halves = []


def _swa(u, sink2):
    b, s, _ = u.shape
    blk = lambda name, w: U_COL[name] // w
    halo_blocks = SWA_TS // SWA_BLOCK
    return pl.pallas_call(
        _swa_kernel,
        grid=(b, s // SWA_TS),
        in_specs=[
            pl.BlockSpec((1, SWA_TS, SWA_WIDTH), lambda bi, i: (bi, i, blk("b_q", SWA_WIDTH))),
            pl.BlockSpec((1, SWA_TS, SWA_WIDTH), lambda bi, i: (bi, i, blk("b_z", SWA_WIDTH))),
            pl.BlockSpec((1, SWA_TS, 2 * SWA_KV_WIDTH),
                         lambda bi, i: (bi, i, blk("b_kv", 2 * SWA_KV_WIDTH))),
            pl.BlockSpec((1, SWA_BLOCK, 2 * SWA_KV_WIDTH),
                         lambda bi, i: (bi, jnp.maximum(i * halo_blocks - 1, 0),
                                        blk("b_kv", 2 * SWA_KV_WIDTH))),
            pl.BlockSpec((SWA_KV_HEADS, SWA_TILES * SWA_BLOCK), lambda bi, i: (0, 0)),
        ],
        out_specs=pl.BlockSpec((1, SWA_TS, SWA_WIDTH), lambda bi, i: (bi, i, 0)),
        out_shape=jax.ShapeDtypeStruct((b, s, SWA_WIDTH), BF16),
        compiler_params=pltpu.CompilerParams(
            dimension_semantics=("arbitrary", "arbitrary"),
            vmem_limit_bytes=V7X_VMEM_LIMIT_BYTES),
        name="swa",
    )(u, u, u, u, sink2)


MOBA_PAIRS = MOBA_WIDTH // LANES
MOBA_SEL_CHUNK = 2048
MOBA_TQ = 512
MOBA_G = MOBA_TQ // MOBA_BLOCK
MOBA_TILES = 4
MOBA_UNROLL = 4
POS_BIG = 1e30
TILE_ORDER = tuple((h, b) for h in range(2) for b in range(MOBA_G))
PREFETCH_AFTER = ((TILE_ORDER[0], TILE_ORDER[1]), (TILE_ORDER[2],), (TILE_ORDER[3],), ())


def _head_split(q):
    lane = lax.broadcasted_iota(jnp.int32, q.shape, 1)
    return (jnp.where(lane < HEAD_DIM, q, 0), jnp.where(lane >= HEAD_DIM, q, 0))


def _moba_select_kernel(q_ref, k_ref, sel_ref, kmean_ref, *, nblk):
    s = k_ref.shape[1]

    def mean_body(j, carry):
        off = pl.multiple_of(j * MOBA_BLOCK, MOBA_BLOCK)
        kj = k_ref[0, pl.ds(off, MOBA_BLOCK), :].astype(F32)
        kmean_ref[pl.ds(j, 1), :] = jnp.sum(kj, axis=0, keepdims=True) * (1.0 / MOBA_BLOCK)
        return carry
    lax.fori_loop(0, nblk, mean_body, 0)

    km = kmean_ref[...]
    km_hi = km.astype(BF16)
    km_lo = (km - km_hi.astype(F32)).astype(BF16)
    ch = MOBA_SEL_CHUNK
    blk_id = lax.broadcasted_iota(jnp.int32, (nblk, ch), 0)
    blk_f = blk_id.astype(F32)
    col = lax.broadcasted_iota(jnp.int32, (nblk, ch), 1)

    def chunk_body(c, carry):
        off = pl.multiple_of(c * ch, ch)
        q_heads = _head_split(q_ref[0, pl.ds(off, ch), :])
        past = blk_id < lax.shift_right_logical(off + col, MOBA_BLOCK.bit_length() - 1)
        for h in range(2):
            g = (lax.dot_general(km_hi, q_heads[h], _NT, preferred_element_type=F32)
                 + lax.dot_general(km_lo, q_heads[h], _NT, preferred_element_type=F32))
            g = jnp.where(past, g, NEG_INF)
            sel = jnp.zeros((nblk, ch), jnp.bool_)
            for _ in range(MOBA_TOPK):
                mx = jnp.max(g, axis=0, keepdims=True)
                first = jnp.min(jnp.where(g == mx, blk_f, float(nblk)), axis=0, keepdims=True)
                pick = blk_f == first
                sel = sel | pick
                g = jnp.where(pick, -jnp.inf, g)
            sel_ref[0, h, :, pl.ds(off, ch)] = jnp.where(sel & past, 1.0, 0.0)
        return carry
    lax.fori_loop(0, s // ch, chunk_body, 0)


def _moba_select(u):
    b, s, _ = u.shape
    nblk = s // MOBA_BLOCK
    blk = lambda name: U_COL[name] // LANES
    return pl.pallas_call(
        functools.partial(_moba_select_kernel, nblk=nblk),
        grid=(b, MOBA_PAIRS),
        in_specs=[pl.BlockSpec((1, s, LANES), lambda bi, p: (bi, 0, blk("c_q") + p)),
                  pl.BlockSpec((1, s, LANES), lambda bi, p: (bi, 0, blk("c_k") + p))],
        out_specs=pl.BlockSpec((1, 2, nblk, s), lambda bi, p: (bi, p, 0, 0)),
        out_shape=jax.ShapeDtypeStruct((b, MOBA_HEADS, nblk, s), F32),
        scratch_shapes=[pltpu.VMEM((nblk, LANES), F32)],
        compiler_params=pltpu.CompilerParams(
            dimension_semantics=("arbitrary", "arbitrary"),
            vmem_limit_bytes=V7X_VMEM_LIMIT_BYTES),
        name="moba_select",
    )(u, u)


def _moba_kernel(q_ref, z_ref, sel_ref, k_ref, v_ref, o_ref, sa_ref, sb_ref):
    gk = MOBA_G * MOBA_BLOCK
    blocks = [slice(b * MOBA_BLOCK, (b + 1) * MOBA_BLOCK) for b in range(MOBA_G)]
    sum_rows = jnp.ones((SUM_ROWS, MOBA_BLOCK), BF16)

    def scores(qh, g, h, rows, sc_ref):
        off = pl.multiple_of(g * gk, gk)
        st = lax.dot_general(k_ref[0, pl.ds(off, gk), :][rows], qh[h], _NT,
                             preferred_element_type=F32)
        sc_ref[h, rows, :] = st
        return jnp.max(st, axis=0, keepdims=True)

    def step(g, sc_ref, maxima, on, mask, state, nxt):
        off = pl.multiple_of(g * gk, gk)
        vt = v_ref[0, pl.ds(off, gk), :].T
        masked, m_new = {}, []
        for h in range(2):
            if mask is not None:
                for b, rows in enumerate(blocks):
                    masked[h, b] = jnp.where(mask[rows], sc_ref[h, rows, :], NEG_INF)
                block_max = [jnp.max(masked[h, b], axis=0, keepdims=True) for b in range(MOBA_G)]
            else:
                block_max = maxima[h]
            mh = state[h][0]
            for mb, o in zip(block_max, on[h]):
                mh = jnp.maximum(mh, jnp.where(o > 0.5, mb, NEG_INF))
            m_new.append(mh)

        pv = [None, None]
        next_maxima = [[None] * MOBA_G for _ in range(2)]
        for idx, (h, b) in enumerate(TILE_ORDER):
            rows = blocks[b]
            tile = masked[h, b] if mask is not None else sc_ref[h, rows, :]
            p = jnp.exp2(tile - jnp.where(on[h][b] > 0.5, m_new[h], POS_BIG)).astype(BF16)
            vt_b = jnp.concatenate([vt[h * HEAD_DIM:(h + 1) * HEAD_DIM, rows], sum_rows], axis=0)
            part = jnp.dot(vt_b, p, preferred_element_type=F32)
            pv[h] = part if pv[h] is None else pv[h] + part
            if nxt is not None:
                for hn, bn in PREFETCH_AFTER[idx]:
                    next_maxima[hn][bn] = scores(nxt[0], nxt[1], hn, blocks[bn], nxt[2])

        new_state = []
        for h in range(2):
            m, l, acc = state[h]
            alpha = jnp.exp2(m - m_new[h])
            new_state.append((m_new[h], alpha * l + pv[h][HEAD_DIM:HEAD_DIM + 1],
                              alpha * acc + pv[h][:HEAD_DIM]))
        return tuple(new_state), next_maxima

    def attend(tile, t, qh, bufs, maxima, nxt_qh):
        cols = slice(tile * MOBA_TQ, (tile + 1) * MOBA_TQ)

        def half_step(g, cur_ref, nxt_ref, state, maxima):
            on = [[sel_ref[0, h, pl.ds(g * MOBA_G + b, 1), cols] for b in range(MOBA_G)]
                  for h in range(2)]
            return step(g, cur_ref, maxima, on, None, state, (qh, g + 1, nxt_ref))

        def run(first, count, state, maxima):
            for k in range(count):
                state, maxima = half_step(first + k, bufs[k % 2], bufs[(k + 1) % 2], state, maxima)
            return state, maxima

        init = tuple((jnp.full((1, MOBA_TQ), NEG_INF, F32),
                      jnp.zeros((1, MOBA_TQ), F32),
                      jnp.zeros((HEAD_DIM, MOBA_TQ), F32)) for _ in range(2))
        state, maxima = lax.fori_loop(
            0, t // MOBA_UNROLL, lambda jj, c: run(MOBA_UNROLL * jj, MOBA_UNROLL, *c), (init, maxima))

        def own_group(sc_ref, free_ref, state):
            kr = lax.broadcasted_iota(jnp.int32, (gk, MOBA_TQ), 0)
            qc = lax.broadcasted_iota(jnp.int32, (gk, MOBA_TQ), 1)
            second_half = lax.broadcasted_iota(jnp.int32, (1, MOBA_TQ), 1) >= MOBA_BLOCK
            ones = jnp.ones((1, MOBA_TQ), F32)
            on = [[jnp.where(second_half, sel_ref[0, h, pl.ds(t * MOBA_G, 1), cols], 1.0), ones]
                  for h in range(2)]
            nxt = None if nxt_qh is None else (nxt_qh, 0, free_ref)
            return step(t, sc_ref, None, on, kr <= qc, state, nxt)

        def finish(left):
            def branch(state, maxima):
                state, _ = run(t - left, left, state, maxima)
                state, next_maxima = own_group(bufs[left % 2], bufs[(left + 1) % 2], state)
                return (state, next_maxima) if nxt_qh is not None else state
            return branch

        lefts = [r for r in range(MOBA_UNROLL) if r % MOBA_TILES == tile]
        if len(lefts) == 1:
            out = finish(lefts[0])(state, maxima)
        else:
            out = lax.switch(lax.rem(t, MOBA_UNROLL) // MOBA_TILES, [finish(r) for r in lefts],
                             state, maxima)
        state, next_maxima = out if nxt_qh is not None else (out, None)

        ot = jnp.concatenate([acc / l for (_, l, acc) in state], axis=0)
        z = z_ref[0, cols, :].astype(F32)
        o_ref[0, cols, :] = (ot.T * _silu(z)).astype(BF16)
        return next_maxima

    t0 = MOBA_TILES * pl.program_id(2)
    qhs = [_head_split(q_ref[0, k * MOBA_TQ:(k + 1) * MOBA_TQ, :])
           for k in range(MOBA_TILES)]
    bufs = (sa_ref, sb_ref)
    maxima = [[scores(qhs[0], 0, h, rows, bufs[0]) for rows in blocks] for h in range(2)]
    for k in range(MOBA_TILES):
        nxt_qh = qhs[k + 1] if k + 1 < MOBA_TILES else None
        maxima = attend(k, t0 + k, qhs[k], bufs, maxima, nxt_qh)
        bufs = (bufs[(k + 1) % 2], bufs[k % 2])


def _moba(u, sel):
    b, s, _ = u.shape
    nblk = s // MOBA_BLOCK
    blk = lambda name: U_COL[name] // LANES
    tq = MOBA_TILES * MOBA_TQ
    return pl.pallas_call(
        _moba_kernel,
        grid=(b, MOBA_PAIRS, s // tq),
        in_specs=[
            pl.BlockSpec((1, tq, LANES), lambda bi, p, t: (bi, t, blk("c_q") + p)),
            pl.BlockSpec((1, tq, LANES), lambda bi, p, t: (bi, t, blk("c_z") + p)),
            pl.BlockSpec((1, 2, nblk, tq), lambda bi, p, t: (bi, p, 0, t)),
            pl.BlockSpec((1, s, LANES), lambda bi, p, t: (bi, 0, blk("c_k") + p)),
            pl.BlockSpec((1, s, LANES), lambda bi, p, t: (bi, 0, blk("c_v") + p)),
        ],
        out_specs=pl.BlockSpec((1, tq, LANES), lambda bi, p, t: (bi, t, p)),
        out_shape=jax.ShapeDtypeStruct((b, s, MOBA_WIDTH), BF16),
        scratch_shapes=[pltpu.VMEM((2, MOBA_G * MOBA_BLOCK, MOBA_TQ), F32)] * 2,
        compiler_params=pltpu.CompilerParams(
            dimension_semantics=("arbitrary", "arbitrary", "arbitrary"),
            vmem_limit_bytes=V7X_VMEM_LIMIT_BYTES),
        name="moba",
    )(u, u, sel, u, u)


MERGE_TM = 1024


def _merge_kernel(x_ref, ax_ref, axh_ref, az_ref, yb_ref, yc_ref, ga_ref, gb_ref, gc_ref,
                  pw_ref, ps_ref, wa_ref, wb_ref, wc_ref, wo_ref, fg_ref, o_ref,
                  *, final_norm):
    i = pl.program_id(1)
    tm = MERGE_TM

    def gate(g_ref):
        return jnp.tanh(g_ref[0].astype(F32)) + 1.0

    xe = jnp.concatenate([jnp.where(i > 0, axh_ref[0], 0), ax_ref[0]], axis=0)
    back = (lax.broadcasted_iota(jnp.int32, (POOL_HALO, 2 * POOL_HALO), 0) + POOL_HALO
            - lax.broadcasted_iota(jnp.int32, (POOL_HALO, 2 * POOL_HALO), 1))
    tpos = i * tm + lax.broadcasted_iota(jnp.int32, (tm, POOL_GROUP_DIM), 0)
    ya = []
    for gi, w in enumerate(POOL_WINDOWS):
        cols = slice(gi * POOL_GROUP_DIM, (gi + 1) * POOL_GROUP_DIM)
        band = jnp.where((back >= 0) & (back < w), 1.0, 0.0).astype(BF16)
        win = jnp.concatenate(
            [jnp.dot(band, xe[r0:r0 + 2 * POOL_HALO, cols], preferred_element_type=F32)
             for r0 in range(0, tm, POOL_HALO)], axis=0)
        cnt = jnp.minimum(tpos + 1, w).astype(F32)
        pooled = (win / cnt - ax_ref[0, :, cols].astype(F32)).astype(BF16)
        mixed = jnp.dot(pooled, pw_ref[gi], preferred_element_type=F32) * ps_ref[:, cols]
        ya.append((mixed * _silu(az_ref[0, :, cols].astype(F32))).astype(BF16))
    ya = jnp.concatenate(ya, axis=1)

    merged = gate(ga_ref) * jnp.dot(ya, wa_ref[...], preferred_element_type=F32)
    merged += gate(gb_ref) * jnp.dot(yb_ref[0], wb_ref[...], preferred_element_type=F32)
    merged += gate(gc_ref) * jnp.dot(yc_ref[0], wc_ref[...], preferred_element_type=F32)
    out = x_ref[0] + jnp.dot(merged.astype(BF16), wo_ref[...], preferred_element_type=F32)
    if final_norm:
        ms = jnp.mean(out * out, axis=-1, keepdims=True)
        out = out * lax.rsqrt(ms + RMS_EPS) * fg_ref[...]
    o_ref[0] = out


def _merge(x, u, yb, yc, pool_w, pool_scale, wa, wb, wc, wo, fg, final_norm):
    b, s, d = x.shape
    tm = MERGE_TM
    ublk = lambda name, w: U_COL[name] // w
    gate_blk = U_COL["gates"] // d
    const = lambda shape: pl.BlockSpec(shape, lambda bi, i: (0,) * len(shape))
    row = lambda w, c: pl.BlockSpec((1, tm, w), lambda bi, i: (bi, i, c))
    return pl.pallas_call(
        functools.partial(_merge_kernel, final_norm=final_norm),
        grid=(b, s // tm),
        in_specs=[
            row(d, 0),
            row(POOL_WIDTH, ublk("a_x", POOL_WIDTH)),
            pl.BlockSpec((1, POOL_HALO, POOL_WIDTH),
                         lambda bi, i: (bi, jnp.maximum(i * (tm // POOL_HALO) - 1, 0),
                                        ublk("a_x", POOL_WIDTH))),
            row(POOL_WIDTH, ublk("a_z", POOL_WIDTH)),
            row(SWA_WIDTH, 0),
            row(MOBA_WIDTH, 0),
            row(d, gate_blk), row(d, gate_blk + 1), row(d, gate_blk + 2),
            const((POOL_GROUPS, POOL_GROUP_DIM, POOL_GROUP_DIM)),
            const((1, POOL_WIDTH)),
            const((POOL_WIDTH, d)), const((SWA_WIDTH, d)), const((MOBA_WIDTH, d)),
            const((d, d)),
            const((1, d)),
        ],
        out_specs=row(d, 0),
        out_shape=jax.ShapeDtypeStruct((b, s, d), F32),
        compiler_params=pltpu.CompilerParams(
            dimension_semantics=("arbitrary", "arbitrary"),
            vmem_limit_bytes=V7X_VMEM_LIMIT_BYTES),
        name="merge",
    )(x, u, u, u, yb, yc, u, u, u, pool_w, pool_scale, wa, wb, wc, wo, fg)


def kernel(x, norm_g, w_in, pool_w, pool_scale, sink_logits, w_proj_a, w_proj_b, w_proj_c,
           w_out, final_norm_g):
    depth = w_in.shape[0]
    assert x.shape[1] % max(INPROJ_TM, SWA_TS, MOBA_TILES * MOBA_TQ, MOBA_SEL_CHUNK, MERGE_TM) == 0
    assert x.shape[2] == D_MODEL and w_in.shape[2] == sum(_REF_SIZES)
    fg = final_norm_g.reshape(1, D_MODEL)
    w_u = _u_weights(w_in)
    for l in range(depth):
        sink2 = (sink_logits[l].astype(F32) * LOG2E).reshape(SWA_KV_HEADS, SWA_TILES)
        sink2 = jnp.repeat(sink2, SWA_BLOCK, axis=1)
        wb = _swa_head_major(w_proj_b[l], 0)
        u = _inproj(x, norm_g[l].reshape(1, D_MODEL), w_u, l)
        yb = _swa(u, sink2)
        yc = _moba(u, _moba_select(u))
        x = _merge(x, u, yb, yc,
                   pool_w[l].astype(BF16), pool_scale[l].reshape(1, POOL_WIDTH),
                   w_proj_a[l].astype(BF16), wb.astype(BF16),
                   w_proj_c[l].astype(BF16), (w_out[l] * GATE_SCALE).astype(BF16), fg,
                   final_norm=(l == depth - 1))
    return x
```

```python
import functools

import numpy as np
import jax
import jax.numpy as jnp
from jax import lax
from jax.experimental import pallas as pl
from jax.experimental.pallas import tpu as pltpu

F32 = jnp.float32
BF16 = jnp.bfloat16

LANES = 128
V7X_VMEM_LIMIT_BYTES = 56 * 1024 * 1024

D_MODEL = 1024
HEAD_DIM = 64
RMS_EPS = 1e-6
NEG_INF = -1e30
LOG2E = 1.4426950408889634
ATTN_SCALE = HEAD_DIM ** -0.5

POOL_WIDTH = 512
POOL_GROUPS = 4
POOL_GROUP_DIM = POOL_WIDTH // POOL_GROUPS
POOL_WINDOWS = (2, 4, 8, 16)
POOL_HALO = 128

SWA_Q_HEADS = 8
SWA_KV_HEADS = 2
SWA_BLOCK = 128
SWA_WIDTH = SWA_Q_HEADS * HEAD_DIM
SWA_KV_WIDTH = SWA_KV_HEADS * HEAD_DIM

MOBA_HEADS = 8
MOBA_BLOCK = 256
MOBA_TOPK = 3
MOBA_WIDTH = MOBA_HEADS * HEAD_DIM

N_BRANCH = 3

_REF_SIZES = (POOL_WIDTH, POOL_WIDTH, SWA_WIDTH, SWA_KV_WIDTH, SWA_KV_WIDTH, SWA_WIDTH,
              MOBA_WIDTH, MOBA_WIDTH, MOBA_WIDTH, MOBA_WIDTH, N_BRANCH * D_MODEL)
_REF_OFF = dict(zip(("a_x", "a_z", "b_q", "b_k", "b_v", "b_z", "c_q", "c_k", "c_v", "c_z", "gates"),
                    np.cumsum((0,) + _REF_SIZES[:-1]).tolist()))

U_COL = dict(a_x=0, a_z=512, b_q=1024, b_z=1536, c_q=2048, c_k=2560, c_v=3072, c_z=3584,
             gates=4096, b_kv=7168)
U_WIDTH = 7168 + 2 * SWA_KV_WIDTH

GATE_SCALE = 0.5


def _swa_head_major(a, axis):
    shape = a.shape
    a = a.reshape(shape[:axis] + (SWA_KV_HEADS, SWA_Q_HEADS // SWA_KV_HEADS, HEAD_DIM)
                  + shape[axis + 1:])
    return jnp.swapaxes(a, axis, axis + 1).reshape(shape)


def _u_weights(w):
    qscale = np.float32(ATTN_SCALE * LOG2E)

    def ref(name, n):
        return w[..., _REF_OFF[name]:_REF_OFF[name] + n]

    parts = [ref("a_x", 2 * POOL_WIDTH),
             _swa_head_major(ref("b_q", SWA_WIDTH), 2) * qscale,
             _swa_head_major(ref("b_z", SWA_WIDTH), 2),
             ref("c_q", MOBA_WIDTH) * qscale,
             ref("c_k", 3 * MOBA_WIDTH),
             ref("gates", N_BRANCH * D_MODEL) * np.float32(GATE_SCALE),
             ref("b_k", 2 * SWA_KV_WIDTH)]
    out = jnp.concatenate(parts, axis=-1).astype(BF16)
    assert out.shape[-1] == U_WIDTH
    return out


def _sigmoid(z):
    return 0.5 * jnp.tanh(0.5 * z) + 0.5


def _silu(z):
    return z * _sigmoid(z)


_NT = (((1,), (1,)), ((), ()))


INPROJ_TM = 512
INPROJ_CHUNK = 1024


def _inproj_kernel(x_ref, g_ref, w_ref, u_ref):
    x = x_ref[0]
    ms = jnp.mean(x * x, axis=-1, keepdims=True)
    h = (x * lax.rsqrt(ms + RMS_EPS) * g_ref[...]).astype(BF16)
    for c0 in range(0, U_WIDTH, INPROJ_CHUNK):
        c1 = min(c0 + INPROJ_CHUNK, U_WIDTH)
        u_ref[0, :, c0:c1] = jnp.dot(h, w_ref[0, :, c0:c1],
                                     preferred_element_type=F32).astype(BF16)


def _inproj(x, g, w, layer):
    b, s, d = x.shape
    return pl.pallas_call(
        _inproj_kernel,
        grid=(b, s // INPROJ_TM),
        in_specs=[pl.BlockSpec((1, INPROJ_TM, d), lambda bi, i: (bi, i, 0)),
                  pl.BlockSpec((1, d), lambda bi, i: (0, 0)),
                  pl.BlockSpec((1, d, U_WIDTH), lambda bi, i: (layer, 0, 0))],
        out_specs=pl.BlockSpec((1, INPROJ_TM, U_WIDTH), lambda bi, i: (bi, i, 0)),
        out_shape=jax.ShapeDtypeStruct((b, s, U_WIDTH), BF16),
        compiler_params=pltpu.CompilerParams(
            dimension_semantics=("arbitrary", "arbitrary"),
            vmem_limit_bytes=V7X_VMEM_LIMIT_BYTES),
        name="inproj",
    )(x, g, w)


SWA_TS = 1024
SWA_TILES = SWA_WIDTH // LANES
SWA_LOOKAHEAD = 2
SUM_ROWS = 16


def _swa_kernel(q_ref, z_ref, kv_ref, kvh_ref, sink_ref, o_ref):
    i = pl.program_id(1)
    nq = SWA_TILES * SWA_BLOCK
    lane = lax.broadcasted_iota(jnp.int32, (SWA_BLOCK, LANES), 1)
    half_mask = (lane < HEAD_DIM, lane >= HEAD_DIM)
    r = lax.broadcasted_iota(jnp.int32, (2 * SWA_BLOCK, nq), 0)
    c = lax.broadcasted_iota(jnp.int32, (2 * SWA_BLOCK, nq), 1) & (SWA_BLOCK - 1)
    band = (r > c) & (r <= c + SWA_BLOCK)
    sum_rows = jnp.ones((SUM_ROWS, 2 * SWA_BLOCK), BF16)

    def kv_pair(n):
        kv_prev = kvh_ref[0] if n == 0 else kv_ref[0, (n - 1) * SWA_BLOCK:n * SWA_BLOCK, :]
        return jnp.concatenate([kv_prev, kv_ref[0, n * SWA_BLOCK:(n + 1) * SWA_BLOCK, :]], axis=0)

    def qk(n, half):
        rows = slice(n * SWA_BLOCK, (n + 1) * SWA_BLOCK)
        qs = jnp.concatenate(
            [jnp.where(half_mask[half], q_ref[0, rows, t * LANES:(t + 1) * LANES], 0)
             for t in range(SWA_TILES)], axis=0)
        return lax.dot_general(kv_pair(n)[:, :SWA_KV_WIDTH], qs, _NT,
                               preferred_element_type=F32)

    def softmax_pv(n, half, st):
        valid = band & ((r >= SWA_BLOCK) | (i > 0)) if n == 0 else band
        st = jnp.where(valid, st, NEG_INF)
        sink = sink_ref[half:half + 1, :]
        m = jnp.maximum(jnp.max(st, axis=0, keepdims=True), sink)
        p = jnp.exp2(st - m).astype(BF16)
        vt = kv_pair(n)[:, SWA_KV_WIDTH:].T
        vt_h = jnp.concatenate([vt[half * HEAD_DIM:(half + 1) * HEAD_DIM], sum_rows], axis=0)
        pv = jnp.dot(vt_h, p, preferred_element_type=F32)
        return pv[:HEAD_DIM] / (pv[HEAD_DIM:HEAD_DIM + 1] + jnp.exp2(sink - m))

    def finish(n, halves):
        rows = slice(n * SWA_BLOCK, (n + 1) * SWA_BLOCK)
        ot = jnp.concatenate(halves, axis=0)
        for t in range(SWA_TILES):
            cols = slice(t * LANES, (t + 1) * LANES)
            o = ot[:, cols].T
            z = z_ref[0, rows, cols].astype(F32)
            o_ref[0, rows, cols] = (o * _silu(z)).astype(BF16)

    tiles = [(n, half) for n in range(SWA_TS // SWA_BLOCK) for half in range(SWA_KV_HEADS)]
    pending, halves = {}, []
    for idx in range(len(tiles) + SWA_LOOKAHEAD):
        if idx < len(tiles):
            pending[idx] = qk(*tiles[idx])
        done = idx - SWA_LOOKAHEAD
        if done >= 0:
            n, half = tiles[done]
            halves.append(softmax_pv(n, half, pending.pop(done)))
            if half == SWA_KV_HEADS - 1:
                finish(n, halves)
                halves = []


def _swa(u, sink2):
    b, s, _ = u.shape
    blk = lambda name, w: U_COL[name] // w
    halo_blocks = SWA_TS // SWA_BLOCK
    return pl.pallas_call(
        _swa_kernel,
        grid=(b, s // SWA_TS),
        in_specs=[
            pl.BlockSpec((1, SWA_TS, SWA_WIDTH), lambda bi, i: (bi, i, blk("b_q", SWA_WIDTH))),
            pl.BlockSpec((1, SWA_TS, SWA_WIDTH), lambda bi, i: (bi, i, blk("b_z", SWA_WIDTH))),
            pl.BlockSpec((1, SWA_TS, 2 * SWA_KV_WIDTH),
                         lambda bi, i: (bi, i, blk("b_kv", 2 * SWA_KV_WIDTH))),
            pl.BlockSpec((1, SWA_BLOCK, 2 * SWA_KV_WIDTH),
                         lambda bi, i: (bi, jnp.maximum(i * halo_blocks - 1, 0),
                                        blk("b_kv", 2 * SWA_KV_WIDTH))),
            pl.BlockSpec((SWA_KV_HEADS, SWA_TILES * SWA_BLOCK), lambda bi, i: (0, 0)),
        ],
        out_specs=pl.BlockSpec((1, SWA_TS, SWA_WIDTH), lambda bi, i: (bi, i, 0)),
        out_shape=jax.ShapeDtypeStruct((b, s, SWA_WIDTH), BF16),
        compiler_params=pltpu.CompilerParams(
            dimension_semantics=("arbitrary", "arbitrary"),
            vmem_limit_bytes=V7X_VMEM_LIMIT_BYTES),
        name="swa",
    )(u, u, u, u, sink2)


MOBA_PAIRS = MOBA_WIDTH // LANES
MOBA_SEL_CHUNK = 2048
MOBA_TQ = 512
MOBA_G = MOBA_TQ // MOBA_BLOCK
MOBA_TILES = 4
MOBA_UNROLL = 8
POS_BIG = 1e30
TILE_ORDER = tuple((h, b) for h in range(2) for b in range(MOBA_G))
PREFETCH_AFTER = ((TILE_ORDER[0], TILE_ORDER[1]), (TILE_ORDER[2],), (TILE_ORDER[3],), ())


def _head_split(q):
    lane = lax.broadcasted_iota(jnp.int32, q.shape, 1)
    return (jnp.where(lane < HEAD_DIM, q, 0), jnp.where(lane >= HEAD_DIM, q, 0))


def _moba_select_kernel(q_ref, k_ref, sel_ref, kmean_ref, *, nblk):
    s = k_ref.shape[1]

    def mean_body(j, carry):
        off = pl.multiple_of(j * MOBA_BLOCK, MOBA_BLOCK)
        kj = k_ref[0, pl.ds(off, MOBA_BLOCK), :].astype(F32)
        kmean_ref[pl.ds(j, 1), :] = jnp.sum(kj, axis=0, keepdims=True) * (1.0 / MOBA_BLOCK)
        return carry
    lax.fori_loop(0, nblk, mean_body, 0)

    km = kmean_ref[...]
    km_hi = km.astype(BF16)
    km_lo = (km - km_hi.astype(F32)).astype(BF16)
    ch = MOBA_SEL_CHUNK
    blk_id = lax.broadcasted_iota(jnp.int32, (nblk, ch), 0)
    blk_f = blk_id.astype(F32)
    col = lax.broadcasted_iota(jnp.int32, (nblk, ch), 1)

    def chunk_body(c, carry):
        off = pl.multiple_of(c * ch, ch)
        q_heads = _head_split(q_ref[0, pl.ds(off, ch), :])
        past = blk_id < lax.shift_right_logical(off + col, MOBA_BLOCK.bit_length() - 1)
        for h in range(2):
            g = (lax.dot_general(km_hi, q_heads[h], _NT, preferred_element_type=F32)
                 + lax.dot_general(km_lo, q_heads[h], _NT, preferred_element_type=F32))
            g = jnp.where(past, g, NEG_INF)
            sel = jnp.zeros((nblk, ch), jnp.bool_)
            for _ in range(MOBA_TOPK):
                mx = jnp.max(g, axis=0, keepdims=True)
                first = jnp.min(jnp.where(g == mx, blk_f, float(nblk)), axis=0, keepdims=True)
                pick = blk_f == first
                sel = sel | pick
                g = jnp.where(pick, -jnp.inf, g)
            sel_ref[0, h, :, pl.ds(off, ch)] = jnp.where(sel & past, 1.0, 0.0)
        return carry
    lax.fori_loop(0, s // ch, chunk_body, 0)


def _moba_select(u):
    b, s, _ = u.shape
    nblk = s // MOBA_BLOCK
    blk = lambda name: U_COL[name] // LANES
    return pl.pallas_call(
        functools.partial(_moba_select_kernel, nblk=nblk),
        grid=(b, MOBA_PAIRS),
        in_specs=[pl.BlockSpec((1, s, LANES), lambda bi, p: (bi, 0, blk("c_q") + p)),
                  pl.BlockSpec((1, s, LANES), lambda bi, p: (bi, 0, blk("c_k") + p))],
        out_specs=pl.BlockSpec((1, 2, nblk, s), lambda bi, p: (bi, p, 0, 0)),
        out_shape=jax.ShapeDtypeStruct((b, MOBA_HEADS, nblk, s), F32),
        scratch_shapes=[pltpu.VMEM((nblk, LANES), F32)],
        compiler_params=pltpu.CompilerParams(
            dimension_semantics=("arbitrary", "arbitrary"),
            vmem_limit_bytes=V7X_VMEM_LIMIT_BYTES),
        name="moba_select",
    )(u, u)


def _moba_kernel(q_ref, z_ref, sel_ref, k_ref, v_ref, o_ref, sa_ref, sb_ref):
    gk = MOBA_G * MOBA_BLOCK
    blocks = [slice(b * MOBA_BLOCK, (b + 1) * MOBA_BLOCK) for b in range(MOBA_G)]
    sum_rows = jnp.ones((SUM_ROWS, MOBA_BLOCK), BF16)

    def scores(qh, g, h, rows, sc_ref):
        off = pl.multiple_of(g * gk, gk)
        st = lax.dot_general(k_ref[0, pl.ds(off, gk), :][rows], qh[h], _NT,
                             preferred_element_type=F32)
        sc_ref[h, rows, :] = st
        return jnp.max(st, axis=0, keepdims=True)

    def step(g, sc_ref, maxima, on, mask, state, nxt):
        off = pl.multiple_of(g * gk, gk)
        vt = v_ref[0, pl.ds(off, gk), :].T
        masked, m_new = {}, []
        for h in range(2):
            if mask is not None:
                for b, rows in enumerate(blocks):
                    masked[h, b] = jnp.where(mask[rows], sc_ref[h, rows, :], NEG_INF)
                block_max = [jnp.max(masked[h, b], axis=0, keepdims=True) for b in range(MOBA_G)]
            else:
                block_max = maxima[h]
            mh = state[h][0]
            for mb, o in zip(block_max, on[h]):
                mh = jnp.maximum(mh, jnp.where(o > 0.5, mb, NEG_INF))
            m_new.append(mh)

        pv = [None, None]
        next_maxima = [[None] * MOBA_G for _ in range(2)]
        for idx, (h, b) in enumerate(TILE_ORDER):
            rows = blocks[b]
            tile = masked[h, b] if mask is not None else sc_ref[h, rows, :]
            p = jnp.exp2(tile - jnp.where(on[h][b] > 0.5, m_new[h], POS_BIG)).astype(BF16)
            vt_b = jnp.concatenate([vt[h * HEAD_DIM:(h + 1) * HEAD_DIM, rows], sum_rows], axis=0)
            part = jnp.dot(vt_b, p, preferred_element_type=F32)
            pv[h] = part if pv[h] is None else pv[h] + part
            if nxt is not None:
                for hn, bn in PREFETCH_AFTER[idx]:
                    next_maxima[hn][bn] = scores(nxt[0], nxt[1], hn, blocks[bn], nxt[2])

        new_state = []
        for h in range(2):
            m, l, acc = state[h]
            alpha = jnp.exp2(m - m_new[h])
            new_state.append((m_new[h], alpha * l + pv[h][HEAD_DIM:HEAD_DIM + 1],
                              alpha * acc + pv[h][:HEAD_DIM]))
        return tuple(new_state), next_maxima

    def attend(tile, t, qh, bufs, maxima, nxt_qh):
        cols = slice(tile * MOBA_TQ, (tile + 1) * MOBA_TQ)

        def half_step(g, cur_ref, nxt_ref, state, maxima):
            on = [[sel_ref[0, h, pl.ds(g * MOBA_G + b, 1), cols] for b in range(MOBA_G)]
                  for h in range(2)]
            return step(g, cur_ref, maxima, on, None, state, (qh, g + 1, nxt_ref))

        def run(first, count, state, maxima):
            for k in range(count):
                state, maxima = half_step(first + k, bufs[k % 2], bufs[(k + 1) % 2], state, maxima)
            return state, maxima

        init = tuple((jnp.full((1, MOBA_TQ), NEG_INF, F32),
                      jnp.zeros((1, MOBA_TQ), F32),
                      jnp.zeros((HEAD_DIM, MOBA_TQ), F32)) for _ in range(2))
        state, maxima = lax.fori_loop(
            0, t // MOBA_UNROLL, lambda jj, c: run(MOBA_UNROLL * jj, MOBA_UNROLL, *c), (init, maxima))

        def own_group(sc_ref, free_ref, state):
            kr = lax.broadcasted_iota(jnp.int32, (gk, MOBA_TQ), 0)
            qc = lax.broadcasted_iota(jnp.int32, (gk, MOBA_TQ), 1)
            second_half = lax.broadcasted_iota(jnp.int32, (1, MOBA_TQ), 1) >= MOBA_BLOCK
            ones = jnp.ones((1, MOBA_TQ), F32)
            on = [[jnp.where(second_half, sel_ref[0, h, pl.ds(t * MOBA_G, 1), cols], 1.0), ones]
                  for h in range(2)]
            nxt = None if nxt_qh is None else (nxt_qh, 0, free_ref)
            return step(t, sc_ref, None, on, kr <= qc, state, nxt)

        def finish(left):
            def branch(state, maxima):
                state, _ = run(t - left, left, state, maxima)
                state, next_maxima = own_group(bufs[left % 2], bufs[(left + 1) % 2], state)
                return (state, next_maxima) if nxt_qh is not None else state
            return branch

        lefts = [r for r in range(MOBA_UNROLL) if r % MOBA_TILES == tile]
        if len(lefts) == 1:
            out = finish(lefts[0])(state, maxima)
        else:
            out = lax.switch(lax.rem(t, MOBA_UNROLL) // MOBA_TILES, [finish(r) for r in lefts],
                             state, maxima)
        state, next_maxima = out if nxt_qh is not None else (out, None)

        ot = jnp.concatenate([acc / l for (_, l, acc) in state], axis=0)
        z = z_ref[0, cols, :].astype(F32)
        o_ref[0, cols, :] = (ot.T * _silu(z)).astype(BF16)
        return next_maxima

    t0 = MOBA_TILES * pl.program_id(2)
    qhs = [_head_split(q_ref[0, k * MOBA_TQ:(k + 1) * MOBA_TQ, :])
           for k in range(MOBA_TILES)]
    bufs = (sa_ref, sb_ref)
    maxima = [[scores(qhs[0], 0, h, rows, bufs[0]) for rows in blocks] for h in range(2)]
    for k in range(MOBA_TILES):
        nxt_qh = qhs[k + 1] if k + 1 < MOBA_TILES else None
        maxima = attend(k, t0 + k, qhs[k], bufs, maxima, nxt_qh)
        bufs = (bufs[(k + 1) % 2], bufs[k % 2])


def _moba(u, sel):
    b, s, _ = u.shape
    nblk = s // MOBA_BLOCK
    blk = lambda name: U_COL[name] // LANES
    tq = MOBA_TILES * MOBA_TQ
    return pl.pallas_call(
        _moba_kernel,
        grid=(b, MOBA_PAIRS, s // tq),
        in_specs=[
            pl.BlockSpec((1, tq, LANES), lambda bi, p, t: (bi, t, blk("c_q") + p)),
            pl.BlockSpec((1, tq, LANES), lambda bi, p, t: (bi, t, blk("c_z") + p)),
            pl.BlockSpec((1, 2, nblk, tq), lambda bi, p, t: (bi, p, 0, t)),
            pl.BlockSpec((1, s, LANES), lambda bi, p, t: (bi, 0, blk("c_k") + p)),
            pl.BlockSpec((1, s, LANES), lambda bi, p, t: (bi, 0, blk("c_v") + p)),
        ],
        out_specs=pl.BlockSpec((1, tq, LANES), lambda bi, p, t: (bi, t, p)),
        out_shape=jax.ShapeDtypeStruct((b, s, MOBA_WIDTH), BF16),
        scratch_shapes=[pltpu.VMEM((2, MOBA_G * MOBA_BLOCK, MOBA_TQ), F32)] * 2,
        compiler_params=pltpu.CompilerParams(
            dimension_semantics=("arbitrary", "arbitrary", "arbitrary"),
            vmem_limit_bytes=V7X_VMEM_LIMIT_BYTES),
        name="moba",
    )(u, u, sel, u, u)


MERGE_TM = 1024


def _merge_kernel(x_ref, ax_ref, axh_ref, az_ref, yb_ref, yc_ref, ga_ref, gb_ref, gc_ref,
                  pw_ref, ps_ref, wa_ref, wb_ref, wc_ref, wo_ref, fg_ref, o_ref,
                  *, final_norm):
    i = pl.program_id(1)
    tm = MERGE_TM

    def gate(g_ref):
        return jnp.tanh(g_ref[0].astype(F32)) + 1.0

    xe = jnp.concatenate([jnp.where(i > 0, axh_ref[0], 0), ax_ref[0]], axis=0)
    back = (lax.broadcasted_iota(jnp.int32, (POOL_HALO, 2 * POOL_HALO), 0) + POOL_HALO
            - lax.broadcasted_iota(jnp.int32, (POOL_HALO, 2 * POOL_HALO), 1))
    tpos = i * tm + lax.broadcasted_iota(jnp.int32, (tm, POOL_GROUP_DIM), 0)
    ya = []
    for gi, w in enumerate(POOL_WINDOWS):
        cols = slice(gi * POOL_GROUP_DIM, (gi + 1) * POOL_GROUP_DIM)
        band = jnp.where((back >= 0) & (back < w), 1.0, 0.0).astype(BF16)
        win = jnp.concatenate(
            [jnp.dot(band, xe[r0:r0 + 2 * POOL_HALO, cols], preferred_element_type=F32)
             for r0 in range(0, tm, POOL_HALO)], axis=0)
        cnt = jnp.minimum(tpos + 1, w).astype(F32)
        pooled = (win / cnt - ax_ref[0, :, cols].astype(F32)).astype(BF16)
        mixed = jnp.dot(pooled, pw_ref[gi], preferred_element_type=F32) * ps_ref[:, cols]
        ya.append((mixed * _silu(az_ref[0, :, cols].astype(F32))).astype(BF16))
    ya = jnp.concatenate(ya, axis=1)

    merged = gate(ga_ref) * jnp.dot(ya, wa_ref[...], preferred_element_type=F32)
    merged += gate(gb_ref) * jnp.dot(yb_ref[0], wb_ref[...], preferred_element_type=F32)
    merged += gate(gc_ref) * jnp.dot(yc_ref[0], wc_ref[...], preferred_element_type=F32)
    out = x_ref[0] + jnp.dot(merged.astype(BF16), wo_ref[...], preferred_element_type=F32)
    if final_norm:
        ms = jnp.mean(out * out, axis=-1, keepdims=True)
        out = out * lax.rsqrt(ms + RMS_EPS) * fg_ref[...]
    o_ref[0] = out


def _merge(x, u, yb, yc, pool_w, pool_scale, wa, wb, wc, wo, fg, final_norm):
    b, s, d = x.shape
    tm = MERGE_TM
    ublk = lambda name, w: U_COL[name] // w
    gate_blk = U_COL["gates"] // d
    const = lambda shape: pl.BlockSpec(shape, lambda bi, i: (0,) * len(shape))
    row = lambda w, c: pl.BlockSpec((1, tm, w), lambda bi, i: (bi, i, c))
    return pl.pallas_call(
        functools.partial(_merge_kernel, final_norm=final_norm),
        grid=(b, s // tm),
        in_specs=[
            row(d, 0),
            row(POOL_WIDTH, ublk("a_x", POOL_WIDTH)),
            pl.BlockSpec((1, POOL_HALO, POOL_WIDTH),
                         lambda bi, i: (bi, jnp.maximum(i * (tm // POOL_HALO) - 1, 0),
                                        ublk("a_x", POOL_WIDTH))),
            row(POOL_WIDTH, ublk("a_z", POOL_WIDTH)),
            row(SWA_WIDTH, 0),
            row(MOBA_WIDTH, 0),
            row(d, gate_blk), row(d, gate_blk + 1), row(d, gate_blk + 2),
            const((POOL_GROUPS, POOL_GROUP_DIM, POOL_GROUP_DIM)),
            const((1, POOL_WIDTH)),
            const((POOL_WIDTH, d)), const((SWA_WIDTH, d)), const((MOBA_WIDTH, d)),
            const((d, d)),
            const((1, d)),
        ],
        out_specs=row(d, 0),
        out_shape=jax.ShapeDtypeStruct((b, s, d), F32),
        compiler_params=pltpu.CompilerParams(
            dimension_semantics=("arbitrary", "arbitrary"),
            vmem_limit_bytes=V7X_VMEM_LIMIT_BYTES),
        name="merge",
    )(x, u, u, u, yb, yc, u, u, u, pool_w, pool_scale, wa, wb, wc, wo, fg)


def kernel(x, norm_g, w_in, pool_w, pool_scale, sink_logits, w_proj_a, w_proj_b, w_proj_c,
           w_out, final_norm_g):
    depth = w_in.shape[0]
    assert x.shape[1] % max(INPROJ_TM, SWA_TS, MOBA_TILES * MOBA_TQ, MOBA_SEL_CHUNK, MERGE_TM) == 0
    assert x.shape[2] == D_MODEL and w_in.shape[2] == sum(_REF_SIZES)
    fg = final_norm_g.reshape(1, D_MODEL)
    w_u = _u_weights(w_in)
    for l in range(depth):
        sink2 = (sink_logits[l].astype(F32) * LOG2E).reshape(SWA_KV_HEADS, SWA_TILES)
        sink2 = jnp.repeat(sink2, SWA_BLOCK, axis=1)
        wb = _swa_head_major(w_proj_b[l], 0)
        u = _inproj(x, norm_g[l].reshape(1, D_MODEL), w_u, l)
        yb = _swa(u, sink2)
        yc = _moba(u, _moba_select(u))
        x = _merge(x, u, yb, yc,
                   pool_w[l].astype(BF16), pool_scale[l].reshape(1, POOL_WIDTH),
                   w_proj_a[l].astype(BF16), wb.astype(BF16),
                   w_proj_c[l].astype(BF16), (w_out[l] * GATE_SCALE).astype(BF16), fg,
                   final_norm=(l == depth - 1))
    return x
```
